```python
import jax, jax.numpy as jnp
from jax import lax
import numpy as np

D_MODEL = 1024
BATCH = 2
SEQ = 16384
DEPTH = 2

N_MIXERS = 2
N_GLA = (DEPTH + 1) // N_MIXERS
N_RWKV = DEPTH // N_MIXERS

GLA_HEADS = 4
GLA_KEY_WIDTH = D_MODEL // 2
GLA_VAL_WIDTH = D_MODEL
GLA_DK = GLA_KEY_WIDTH // GLA_HEADS
GLA_DV = GLA_VAL_WIDTH // GLA_HEADS
GLA_GATE_RANK = 16
GLA_GATE_NORMALIZER = 16.0
GLA_CHUNK = 64
GLA_IN = 2 * GLA_KEY_WIDTH + 2 * GLA_VAL_WIDTH + GLA_GATE_RANK

RWKV_HEAD = 64
RWKV_WIDTH = D_MODEL
RWKV_HEADS = RWKV_WIDTH // RWKV_HEAD
RWKV_DECAY_RANK = 64
RWKV_ICLR_RANK = 64
RWKV_IN = 4 * RWKV_WIDTH + RWKV_DECAY_RANK + RWKV_ICLR_RANK

RMS_EPS = 1e-6
GN_EPS = 64e-5

kernel_name = "hybrid_gla_rwkv7_interleaved"


def rmsnorm(x, g):
    xf = x.astype(jnp.float32)
    y = xf * lax.rsqrt(jnp.mean(xf * xf, axis=-1, keepdims=True) + RMS_EPS)
    return (y * g.astype(jnp.float32)).astype(x.dtype)


def gla_mixer(h, w_in, w_alpha_up, b_alpha, g_head, w_out):
    B, T, _ = h.shape
    nc = T // GLA_CHUNK
    p = h @ w_in
    q, k, v, gate, ad = jnp.split(p, [GLA_KEY_WIDTH, 2 * GLA_KEY_WIDTH,
                                       2 * GLA_KEY_WIDTH + GLA_VAL_WIDTH,
                                       2 * GLA_KEY_WIDTH + 2 * GLA_VAL_WIDTH], axis=-1)
    f32 = jnp.float32
    log_alpha = jax.nn.log_sigmoid((ad.astype(f32) @ w_alpha_up.astype(f32)
                                    + b_alpha.astype(f32))) / GLA_GATE_NORMALIZER

    def heads(z, d):
        return z.astype(f32).reshape(B, nc, GLA_CHUNK, GLA_HEADS, d).transpose(0, 3, 1, 2, 4)

    q = heads(q, GLA_DK) * (GLA_DK ** -0.5)
    k = heads(k, GLA_DK)
    v = heads(v, GLA_DV)
    bcum = jnp.cumsum(heads(log_alpha, GLA_DK), axis=3)
    b_last = bcum[:, :, :, -1, :]
    q_dec = q * jnp.exp(bcum)
    k_dec = k * jnp.exp(-bcum)
    k_to_end = k * jnp.exp(b_last[:, :, :, None, :] - bcum)

    mask = jnp.tril(jnp.ones((GLA_CHUNK, GLA_CHUNK), dtype=bool))
    att = jnp.where(mask, jnp.einsum('bhncd,bhnsd->bhncs', q_dec, k_dec), 0.0)
    o_intra = jnp.einsum('bhncs,bhnsv->bhncv', att, v)

    def step(S, inp):
        qd, kt, vv, dl = inp
        o = jnp.einsum('bhcd,bhdv->bhcv', qd, S)
        S = S * jnp.exp(dl)[..., None] + jnp.einsum('bhcd,bhcv->bhdv', kt, vv)
        return S, o

    S0 = jnp.zeros((B, GLA_HEADS, GLA_DK, GLA_DV), f32)
    xs = (jnp.moveaxis(q_dec, 2, 0), jnp.moveaxis(k_to_end, 2, 0),
          jnp.moveaxis(v, 2, 0), jnp.moveaxis(b_last, 2, 0))
    _, o_inter = lax.scan(step, S0, xs)
    o = o_intra + jnp.moveaxis(o_inter, 0, 2)

    o = o * lax.rsqrt(jnp.mean(o * o, axis=-1, keepdims=True) + RMS_EPS) * g_head.astype(f32)
    o = o.transpose(0, 2, 3, 1, 4).reshape(B, T, GLA_VAL_WIDTH)
    o = o * jax.nn.silu(gate.astype(f32))
    return o.astype(h.dtype) @ w_out


def token_shift(p):
    return jnp.pad(p, ((0, 0), (1, 0), (0, 0)))[:, :-1]


def rwkv7_mixer(h, w_in, mu, w0, w_decay_up, a0, w_iclr_up, k_k, k_a, r_k, ln_w, ln_b, w_out):
    B, T, _ = h.shape
    f32 = jnp.float32
    p = h @ w_in
    p = p + mu * (token_shift(p) - p)
    r, k, v, gate, wd, ad = jnp.split(p, [RWKV_WIDTH, 2 * RWKV_WIDTH, 3 * RWKV_WIDTH,
                                           4 * RWKV_WIDTH, 4 * RWKV_WIDTH + RWKV_DECAY_RANK], axis=-1)
    r, k, v, gate, wd, ad = (z.astype(f32) for z in (r, k, v, gate, wd, ad))

    w_log = -jax.nn.softplus(-(w0.astype(f32) + jnp.tanh(wd) @ w_decay_up.astype(f32))) - 0.5
    decay = jnp.exp(-jnp.exp(w_log))
    a = jax.nn.sigmoid(a0.astype(f32) + ad @ w_iclr_up.astype(f32))

    def heads(z):
        return z.reshape(B, T, RWKV_HEADS, RWKV_HEAD)

    kk = heads(k * k_k.astype(f32))
    kk = kk / jnp.maximum(jnp.sqrt(jnp.sum(kk * kk, axis=-1, keepdims=True)), 1e-12)
    k = k * (1.0 + (a - 1.0) * k_a.astype(f32))
    r, decay, k, v, a = heads(r), heads(decay), heads(k), heads(v), heads(a)

    def step(S, inp):
        r_t, w_t, k_t, v_t, kk_t, a_t = inp
        sa = jnp.einsum('bhvk,bhk->bhv', S, -kk_t)
        S = (S * w_t[:, :, None, :] + sa[..., None] * (kk_t * a_t)[:, :, None, :]
             + v_t[..., None] * k_t[:, :, None, :])
        y = jnp.einsum('bhvk,bhk->bhv', S, r_t)
        return S, y

    S0 = jnp.zeros((B, RWKV_HEADS, RWKV_HEAD, RWKV_HEAD), f32)
    xs = tuple(jnp.moveaxis(z, 1, 0) for z in (r, decay, k, v, kk, a))
    _, y = lax.scan(step, S0, xs)
    y = jnp.moveaxis(y, 0, 1)

    mean = jnp.mean(y, axis=-1, keepdims=True)
    var = jnp.mean(jnp.square(y - mean), axis=-1, keepdims=True)
    y = ((y - mean) * lax.rsqrt(var + GN_EPS)).reshape(B, T, RWKV_WIDTH)
    y = y * ln_w.astype(f32) + ln_b.astype(f32)
    bonus = jnp.sum(r * k * r_k.astype(f32), axis=-1, keepdims=True) * v
    y = y + bonus.reshape(B, T, RWKV_WIDTH)
    y = y * jax.nn.silu(gate)
    return y.astype(h.dtype) @ w_out


def setup_inputs(seed: int = 0) -> dict:
    key = jax.random.key(seed)
    ks = jax.random.split(key, 24)
    nrm = jax.random.normal
    f32 = jnp.float32
    D = D_MODEL
    x = nrm(ks[0], (BATCH, SEQ, D), f32)
    gla_pre_norm = 1.0 + 0.02 * nrm(ks[1], (N_GLA, D), f32)
    gla_w_in = nrm(ks[2], (N_GLA, D, GLA_IN), f32) * D ** -0.5
    gla_w_alpha_up = nrm(ks[3], (N_GLA, GLA_GATE_RANK, GLA_KEY_WIDTH), f32) * GLA_GATE_RANK ** -0.5
    gla_b_alpha = 0.1 * nrm(ks[4], (N_GLA, GLA_KEY_WIDTH), f32)
    gla_head_norm = 1.0 + 0.02 * nrm(ks[5], (N_GLA, GLA_DV), f32)
    gla_w_out = nrm(ks[6], (N_GLA, GLA_VAL_WIDTH, D), f32) * GLA_VAL_WIDTH ** -0.5
    gla_post_norm = 1.0 + 0.02 * nrm(ks[7], (N_GLA, D), f32)
    rwkv_pre_norm = 1.0 + 0.02 * nrm(ks[8], (N_RWKV, D), f32)
    rwkv_w_in = nrm(ks[9], (N_RWKV, D, RWKV_IN), f32) * D ** -0.5
    rwkv_mu = jax.random.uniform(ks[10], (N_RWKV, RWKV_IN), f32)
    rwkv_w0 = -1.0 + 0.5 * nrm(ks[11], (N_RWKV, RWKV_WIDTH), f32)
    rwkv_w_decay_up = 0.1 * nrm(ks[12], (N_RWKV, RWKV_DECAY_RANK, RWKV_WIDTH), f32)
    rwkv_a0 = 0.1 * nrm(ks[13], (N_RWKV, RWKV_WIDTH), f32)
    rwkv_w_iclr_up = 0.1 * nrm(ks[14], (N_RWKV, RWKV_ICLR_RANK, RWKV_WIDTH), f32)
    rwkv_k_k = 0.85 + 0.05 * nrm(ks[15], (N_RWKV, RWKV_WIDTH), f32)
    rwkv_k_a = 1.0 + 0.05 * nrm(ks[16], (N_RWKV, RWKV_WIDTH), f32)
    rwkv_r_k = 0.1 * nrm(ks[17], (N_RWKV, RWKV_HEADS, RWKV_HEAD), f32)
    rwkv_ln_w = 1.0 + 0.02 * nrm(ks[18], (N_RWKV, RWKV_WIDTH), f32)
    rwkv_ln_b = 0.02 * nrm(ks[19], (N_RWKV, RWKV_WIDTH), f32)
    rwkv_w_out = nrm(ks[20], (N_RWKV, RWKV_WIDTH, D), f32) * RWKV_WIDTH ** -0.5
    rwkv_post_norm = 1.0 + 0.02 * nrm(ks[21], (N_RWKV, D), f32)
    return {"x": x,
            "gla_pre_norm": gla_pre_norm, "gla_w_in": gla_w_in, "gla_w_alpha_up": gla_w_alpha_up,
            "gla_b_alpha": gla_b_alpha, "gla_head_norm": gla_head_norm, "gla_w_out": gla_w_out,
            "gla_post_norm": gla_post_norm,
            "rwkv_pre_norm": rwkv_pre_norm, "rwkv_w_in": rwkv_w_in, "rwkv_mu": rwkv_mu,
            "rwkv_w0": rwkv_w0, "rwkv_w_decay_up": rwkv_w_decay_up, "rwkv_a0": rwkv_a0,
            "rwkv_w_iclr_up": rwkv_w_iclr_up, "rwkv_k_k": rwkv_k_k, "rwkv_k_a": rwkv_k_a,
            "rwkv_r_k": rwkv_r_k, "rwkv_ln_w": rwkv_ln_w, "rwkv_ln_b": rwkv_ln_b,
            "rwkv_w_out": rwkv_w_out, "rwkv_post_norm": rwkv_post_norm}


def reference(x, gla_pre_norm, gla_w_in, gla_w_alpha_up, gla_b_alpha, gla_head_norm, gla_w_out,
              gla_post_norm, rwkv_pre_norm, rwkv_w_in, rwkv_mu, rwkv_w0, rwkv_w_decay_up, rwkv_a0,
              rwkv_w_iclr_up, rwkv_k_k, rwkv_k_a, rwkv_r_k, rwkv_ln_w, rwkv_ln_b, rwkv_w_out,
              rwkv_post_norm):
    for i in range(DEPTH):
        j = i // N_MIXERS
        if i % N_MIXERS == 0:
            h = rmsnorm(x, gla_pre_norm[j])
            y = gla_mixer(h, gla_w_in[j], gla_w_alpha_up[j], gla_b_alpha[j], gla_head_norm[j],
                          gla_w_out[j])
            x = x + rmsnorm(y, gla_post_norm[j])
        else:
            h = rmsnorm(x, rwkv_pre_norm[j])
            y = rwkv7_mixer(h, rwkv_w_in[j], rwkv_mu[j], rwkv_w0[j], rwkv_w_decay_up[j], rwkv_a0[j],
                            rwkv_w_iclr_up[j], rwkv_k_k[j], rwkv_k_a[j], rwkv_r_k[j], rwkv_ln_w[j],
                            rwkv_ln_b[j], rwkv_w_out[j])
            x = x + rmsnorm(y, rwkv_post_norm[j])
    return x
```

```python
import functools
import math

import jax
import jax.numpy as jnp
from jax import lax
from jax.experimental import pallas as pl
from jax.experimental.pallas import tpu as pltpu

F32 = jnp.float32
BF16 = jnp.bfloat16

LANES_V7X = 128
MXU_DIM_V7X = 256
VMEM_LIMIT_BYTES = 56 * 1024 * 1024

RMS_EPS = 1e-6
GN_EPS = 64e-5

CHUNK = 64
TILE_T = 256

GLA_HEADS = 4
GLA_DK = 128
GLA_DV = 256
GLA_RANK = 16
GLA_NORMALIZER = 16.0

RWKV_HEAD = 64
GROUP_HEADS = MXU_DIM_V7X // RWKV_HEAD
GROUP_LANES = GROUP_HEADS * RWKV_HEAD
RWKV_RANK = 64


def _dot(a, b):
    return jnp.dot(a, b, preferred_element_type=F32)


def _dot_nt(a, b):
    return lax.dot_general(a, b, (((1,), (1,)), ((), ())), preferred_element_type=F32)


def _dot_tn(a, b):
    return lax.dot_general(a, b, (((0,), (0,)), ((), ())), preferred_element_type=F32)


def _split3(x):
    x1 = x.astype(BF16)
    r1 = x - x1.astype(F32)
    x2 = r1.astype(BF16)
    r2 = r1 - x2.astype(F32)
    return x1, x2, r2.astype(BF16)


def _dot_01_lhs(m01, x):
    x1, x2, x3 = _split3(x)
    return _dot(m01, x1) + _dot(m01, x2) + _dot(m01, x3)


def _dot_01_rhs(x, m01):
    x1, x2, x3 = _split3(x)
    return _dot(x1, m01) + _dot(x2, m01) + _dot(x3, m01)


def _rmsnorm(x, g):
    ms = jnp.mean(x * x, axis=-1, keepdims=True)
    return x * lax.rsqrt(ms + RMS_EPS) * g


def _sigmoid(z):
    return 1.0 / (1.0 + jnp.exp(-z))


def _chunk_tri(tile):
    r = lax.broadcasted_iota(jnp.int32, (tile, tile), 0)
    c = lax.broadcasted_iota(jnp.int32, (tile, tile), 1)
    return ((r // CHUNK == c // CHUNK) & (c <= r)).astype(BF16)


def _chunk_total(cum, tile):
    nc = tile // CHUNK
    width = cum.shape[-1]
    c3 = cum.reshape(nc, CHUNK, width)
    last = c3[:, CHUNK - 1:CHUNK, :]
    return jnp.broadcast_to(last, (nc, CHUNK, width)).reshape(tile, width)


def _gla_kernel(x_ref, pre_g_ref, wq_ref, wk_ref, wv_ref, wg_ref, wad_ref, wup_ref, b_ref,
                ghead_ref, wout_ref, post_g_ref, o_ref, st_ref, o_acc_ref):
    tile = x_ref.shape[1]
    nc = tile // CHUNK

    @pl.when(pl.program_id(1) == 0)
    def _():
        st_ref[...] = jnp.zeros_like(st_ref)

    x = x_ref[0]
    h = _rmsnorm(x, pre_g_ref[...]).astype(BF16)
    q = _dot(h, wq_ref[...])
    k = _dot(h, wk_ref[...])
    v = _dot(h, wv_ref[...])
    gate = _dot(h, wg_ref[...])
    ad = _dot(h, wad_ref[...])
    z = _dot(ad.astype(BF16), wup_ref[...]) + b_ref[...]
    log_alpha = (jnp.minimum(z, 0.0) - jnp.log1p(jnp.exp(-jnp.abs(z)))) * (1.0 / GLA_NORMALIZER)

    bcum = _dot_01_lhs(_chunk_tri(tile), log_alpha)
    btot = _chunk_total(bcum, tile)
    q_dec = (q * (GLA_DK ** -0.5) * jnp.exp(bcum)).astype(BF16)
    k_dec = (k * jnp.exp(-bcum)).astype(BF16)
    k_end = (k * jnp.exp(btot - bcum)).astype(BF16)
    chunk_decay = jnp.exp(btot)
    vb = v.astype(BF16)

    row = lax.broadcasted_iota(jnp.int32, (CHUNK, CHUNK), 0)
    col = lax.broadcasted_iota(jnp.int32, (CHUNK, CHUNK), 1)
    causal = col <= row

    for c in range(nc):
        rows = slice(c * CHUNK, (c + 1) * CHUNK)
        for hd in range(GLA_HEADS):
            kl = slice(hd * GLA_DK, (hd + 1) * GLA_DK)
            vl = slice(hd * GLA_DV, (hd + 1) * GLA_DV)
            qd = q_dec[rows, kl]
            att = jnp.where(causal, _dot_nt(qd, k_dec[rows, kl]), 0.0).astype(BF16)
            st = st_ref[hd]
            o = _dot(att, vb[rows, vl]) + _dot_nt(qd, st.astype(BF16))
            o_acc_ref[rows, vl] = o
            st_ref[hd] = st * chunk_decay[c * CHUNK:c * CHUNK + 1, kl] + _dot_tn(vb[rows, vl], k_end[rows, kl])

    o = o_acc_ref[...]
    outs = []
    for hd in range(GLA_HEADS):
        oh = o[:, hd * GLA_DV:(hd + 1) * GLA_DV]
        outs.append(oh * lax.rsqrt(jnp.mean(oh * oh, axis=-1, keepdims=True) + RMS_EPS) * ghead_ref[...])
    o = jnp.concatenate(outs, axis=-1)
    o = o * (gate * _sigmoid(gate))
    y = _dot(o.astype(BF16), wout_ref[...])
    o_ref[0] = x + _rmsnorm(y, post_g_ref[...])


def _const_spec(shape):
    nd = len(shape)
    return pl.BlockSpec(shape, lambda b, t: (0,) * nd)


def _gla_layer(x, pre_g, w_in, w_up, b_alpha, g_head, w_out, post_g):
    B, T, D = x.shape
    kw = GLA_HEADS * GLA_DK
    vw = GLA_HEADS * GLA_DV
    wq = w_in[:, :kw].astype(BF16)
    wk = w_in[:, kw:2 * kw].astype(BF16)
    wv = w_in[:, 2 * kw:2 * kw + vw].astype(BF16)
    wg = w_in[:, 2 * kw + vw:2 * kw + 2 * vw].astype(BF16)
    wad = jnp.pad(w_in[:, 2 * kw + 2 * vw:], ((0, 0), (0, LANES_V7X - GLA_RANK))).astype(BF16)
    wup = jnp.pad(w_up, ((0, LANES_V7X - GLA_RANK), (0, 0))).astype(BF16)
    args = (x, pre_g.reshape(1, D), wq, wk, wv, wg, wad, wup, b_alpha.reshape(1, kw),
            g_head.reshape(1, GLA_DV), w_out.astype(BF16), post_g.reshape(1, D))
    tile_spec = pl.BlockSpec((1, TILE_T, D), lambda b, t: (b, t, 0))
    in_specs = [tile_spec] + [_const_spec(a.shape) for a in args[1:]]
    return pl.pallas_call(
        _gla_kernel,
        grid=(B, T // TILE_T),
        in_specs=in_specs,
        out_specs=tile_spec,
        out_shape=jax.ShapeDtypeStruct((B, T, D), x.dtype),
        scratch_shapes=[pltpu.VMEM((GLA_HEADS, GLA_DV, GLA_DK), F32),
                        pltpu.VMEM((TILE_T, vw), F32)],
        compiler_params=pltpu.CompilerParams(
            dimension_semantics=("arbitrary", "arbitrary"),
            vmem_limit_bytes=VMEM_LIMIT_BYTES),
        name="gla_layer",
    )(*args)


def _block_rows(x, bd_mask):
    return jnp.where(bd_mask, jnp.concatenate([x] * GROUP_HEADS, axis=0), jnp.zeros((), x.dtype))


def _rwkv_kernel(x_ref, pre_g_ref, wmain_ref, wlo_ref, mu_main_ref, mu_lo_ref, wup_ref, w0_ref, a0_ref,
                 kk_ref, ka_ref, rk_ref, lnw_ref, lnb_ref, gdown_ref, gup_ref, wout_ref, post_g_ref,
                 o_ref,
                 ht_ref, prev_main_ref, prev_lo_ref,
                 at_ref, rt_ref, bt_ref, kt_ref, bh_ref, kh_ref, vb_ref, gam_ref, y_ref):
    tile = x_ref.shape[1]
    width = at_ref.shape[1]
    nc = tile // CHUNK
    ng = width // GROUP_LANES

    @pl.when(pl.program_id(1) == 0)
    def _():
        ht_ref[...] = jnp.zeros_like(ht_ref)
        prev_main_ref[...] = jnp.zeros_like(prev_main_ref)
        prev_lo_ref[...] = jnp.zeros_like(prev_lo_ref)

    x = x_ref[0]
    h = _rmsnorm(x, pre_g_ref[...]).astype(BF16)

    def shifted_lerp(p, prev_ref, mu):
        first = lax.broadcasted_iota(jnp.int32, p.shape, 0) == 0
        shifted = jnp.where(first, prev_ref[...], pltpu.roll(p, 1, 0))
        prev_ref[...] = p[tile - 1:tile, :]
        return p + mu * (shifted - p)

    p_main = shifted_lerp(_dot(h, wmain_ref[...]), prev_main_ref, mu_main_ref[...])
    p_lo = shifted_lerp(_dot(h, wlo_ref[...]), prev_lo_ref, mu_lo_ref[...])
    r = p_main[:, 0 * width:1 * width]
    k = p_main[:, 1 * width:2 * width]
    v = p_main[:, 2 * width:3 * width]
    gate = p_main[:, 3 * width:4 * width]

    lo_lane = lax.broadcasted_iota(jnp.int32, p_lo.shape, 1)
    lo_act = jnp.where(lo_lane < RWKV_RANK, jnp.tanh(p_lo), p_lo).astype(BF16)
    up = _dot(lo_act, wup_ref[...])
    logw = -math.exp(-0.5) * _sigmoid(w0_ref[...] + up[:, :width])
    a = _sigmoid(a0_ref[...] + up[:, width:])

    def head_sum(t):
        return _dot_01_rhs(_dot_01_rhs(t, gdown_ref[...]), gup_ref[...])

    kk = k * kk_ref[...]
    kk = kk / jnp.maximum(jnp.sqrt(head_sum(kk * kk)), 1e-12)
    k = k * (1.0 + (a - 1.0) * ka_ref[...])

    lcum = _dot_01_lhs(_chunk_tri(tile), logw)
    ltot = _chunk_total(lcum, tile)
    kka = kk * a
    e_neg = jnp.exp(-lcum)
    e_end = jnp.exp(ltot - lcum)
    at_ref[...] = (-kk * jnp.exp(lcum - logw)).astype(BF16)
    rt_ref[...] = (r * jnp.exp(lcum)).astype(BF16)
    bt_ref[...] = (kka * e_neg).astype(BF16)
    kt_ref[...] = (k * e_neg).astype(BF16)
    bh_ref[...] = (kka * e_end).astype(BF16)
    kh_ref[...] = (k * e_end).astype(BF16)
    vb_ref[...] = v.astype(BF16)
    gam_ref[...] = jnp.exp(ltot)

    bonus = head_sum(r * k * rk_ref[...]) * v

    br = lax.broadcasted_iota(jnp.int32, (GROUP_LANES, GROUP_LANES), 0)
    bc = lax.broadcasted_iota(jnp.int32, (GROUP_LANES, GROUP_LANES), 1)
    bd_mask = (br // RWKV_HEAD) == (bc // RWKV_HEAD)
    tr = lax.broadcasted_iota(jnp.int32, (CHUNK, GROUP_LANES), 0)
    ts = lax.broadcasted_iota(jnp.int32, (CHUNK, GROUP_LANES), 1) % CHUNK
    strict = ts < tr
    causal = ts <= tr
    eye = (ts == tr).astype(F32)

    for c in range(nc):
        rows = slice(c * CHUNK, (c + 1) * CHUNK)
        for g in range(ng):
            lanes = slice(g * GROUP_LANES, (g + 1) * GROUP_LANES)
            a_t = at_ref[rows, lanes]
            r_t = rt_ref[rows, lanes]
            vv = vb_ref[rows, lanes]
            gram = _dot_nt(jnp.concatenate([a_t, r_t], axis=0),
                           jnp.concatenate([_block_rows(bt_ref[rows, lanes], bd_mask),
                                            _block_rows(kt_ref[rows, lanes], bd_mask)], axis=0))
            l_ab = jnp.where(strict, gram[:CHUNK, :GROUP_LANES], 0.0)
            l_ak = jnp.where(strict, gram[:CHUNK, GROUP_LANES:], 0.0)
            m_rb = jnp.where(causal, gram[CHUNK:, :GROUP_LANES], 0.0)
            m_rk = jnp.where(causal, gram[CHUNK:, GROUP_LANES:], 0.0)

            t_inv = eye + l_ab
            pw = _dot(l_ab.astype(BF16), _block_rows(l_ab.astype(BF16), bd_mask))
            for _ in range(int(math.log2(CHUNK)) - 2):
                both = _dot(jnp.concatenate([t_inv, pw], axis=0).astype(BF16),
                            _block_rows(pw.astype(BF16), bd_mask))
                t_inv = t_inv + both[:CHUNK]
                pw = both[CHUNK:]
            t_inv = t_inv + _dot(t_inv.astype(BF16), _block_rows(pw.astype(BF16), bd_mask))

            l_akv = _dot(l_ak.astype(BF16), _block_rows(vv, bd_mask))
            wu = _dot(t_inv.astype(BF16),
                      jnp.concatenate([_block_rows(a_t, bd_mask),
                                       _block_rows(l_akv.astype(BF16), bd_mask)], axis=1))
            w_mat = wu[:, :GROUP_LANES]
            u0 = wu[:, GROUP_LANES:]

            ht = ht_ref[g]
            wh = _dot_nt(jnp.concatenate([w_mat.astype(BF16), r_t], axis=0), ht.astype(BF16))
            u = (wh[:CHUNK] + u0).astype(BF16)
            y = wh[CHUNK:] + _dot(jnp.concatenate([m_rb, m_rk], axis=1).astype(BF16),
                                  jnp.concatenate([_block_rows(u, bd_mask), _block_rows(vv, bd_mask)], axis=0))
            y_ref[rows, lanes] = y
            upd = _dot_tn(jnp.concatenate([u, vv], axis=0),
                          jnp.concatenate([bh_ref[rows, lanes], kh_ref[rows, lanes]], axis=0))
            ht_ref[g] = ht * gam_ref[c * CHUNK:c * CHUNK + 1, lanes] + jnp.where(bd_mask, upd, 0.0)

    y = y_ref[...]
    mean = head_sum(y) * (1.0 / RWKV_HEAD)
    yc = y - mean
    var = head_sum(yc * yc) * (1.0 / RWKV_HEAD)
    y = yc * lax.rsqrt(var + GN_EPS) * lnw_ref[...] + lnb_ref[...]
    y = (y + bonus) * (gate * _sigmoid(gate))
    out = _dot(y.astype(BF16), wout_ref[...])
    o_ref[0] = x + _rmsnorm(out, post_g_ref[...])


def _rwkv_layer(x, pre_g, w_in, mu, w0, w_decay_up, a0, w_iclr_up, k_k, k_a, r_k, ln_w, ln_b, w_out, post_g):
    B, T, D = x.shape
    width = w_out.shape[0]
    heads = width // RWKV_HEAD
    rank2 = 2 * RWKV_RANK
    wmain = w_in[:, :4 * width].astype(BF16)
    wlo = w_in[:, 4 * width:].astype(BF16)
    zeros = jnp.zeros((RWKV_RANK, width), F32)
    wup = jnp.concatenate([jnp.concatenate([w_decay_up, zeros], axis=1),
                           jnp.concatenate([zeros, w_iclr_up], axis=1)], axis=0).astype(BF16)
    head_of_lane = jnp.arange(width) // RWKV_HEAD
    gdown = (head_of_lane[:, None] == jnp.arange(LANES_V7X)[None, :]).astype(BF16)
    gup = gdown.T
    row = lambda t: t.reshape(1, -1)
    args = (x, row(pre_g), wmain, wlo, row(mu[:4 * width]), row(mu[4 * width:]), wup, row(w0), row(a0),
            row(k_k), row(k_a), row(r_k), row(ln_w), row(ln_b), gdown, gup, w_out.astype(BF16), row(post_g))
    tile_spec = pl.BlockSpec((1, TILE_T, D), lambda b, t: (b, t, 0))
    in_specs = [tile_spec] + [_const_spec(a.shape) for a in args[1:]]
    ngroups = width // GROUP_LANES
    scratch = [pltpu.VMEM((ngroups, GROUP_LANES, GROUP_LANES), F32),
               pltpu.VMEM((1, 4 * width), F32),
               pltpu.VMEM((1, rank2), F32)]
    scratch += [pltpu.VMEM((TILE_T, width), BF16)] * 7
    scratch += [pltpu.VMEM((TILE_T, width), F32)] * 2
    assert heads * RWKV_HEAD == width and rank2 == LANES_V7X
    return pl.pallas_call(
        _rwkv_kernel,
        grid=(B, T // TILE_T),
        in_specs=in_specs,
        out_specs=tile_spec,
        out_shape=jax.ShapeDtypeStruct((B, T, D), x.dtype),
        scratch_shapes=scratch,
        compiler_params=pltpu.CompilerParams(
            dimension_semantics=("arbitrary", "arbitrary"),
            vmem_limit_bytes=VMEM_LIMIT_BYTES),
        name="rwkv_layer",
    )(*args)


def kernel(x, gla_pre_norm, gla_w_in, gla_w_alpha_up, gla_b_alpha, gla_head_norm, gla_w_out, gla_post_norm,
           rwkv_pre_norm, rwkv_w_in, rwkv_mu, rwkv_w0, rwkv_w_decay_up, rwkv_a0, rwkv_w_iclr_up, rwkv_k_k,
           rwkv_k_a, rwkv_r_k, rwkv_ln_w, rwkv_ln_b, rwkv_w_out, rwkv_post_norm):
    depth = gla_pre_norm.shape[0] + rwkv_pre_norm.shape[0]
    for i in range(depth):
        j = i // 2
        if i % 2 == 0:
            x = _gla_layer(x, gla_pre_norm[j], gla_w_in[j], gla_w_alpha_up[j], gla_b_alpha[j],
                           gla_head_norm[j], gla_w_out[j], gla_post_norm[j])
        else:
            x = _rwkv_layer(x, rwkv_pre_norm[j], rwkv_w_in[j], rwkv_mu[j], rwkv_w0[j], rwkv_w_decay_up[j],
                            rwkv_a0[j], rwkv_w_iclr_up[j], rwkv_k_k[j], rwkv_k_a[j], rwkv_r_k[j].reshape(-1),
                            rwkv_ln_w[j], rwkv_ln_b[j], rwkv_w_out[j], rwkv_post_norm[j])
    return x
```

```python
import functools
import math

import jax
import jax.numpy as jnp
from jax import lax
from jax.experimental import pallas as pl
from jax.experimental.pallas import tpu as pltpu

F32 = jnp.float32
BF16 = jnp.bfloat16

LANES_V7X = 128
MXU_DIM_V7X = 256
VMEM_LIMIT_BYTES = 56 * 1024 * 1024

RMS_EPS = 1e-6
GN_EPS = 64e-5

CHUNK = 64
TILE_T = 256

GLA_HEADS = 4
GLA_DK = 128
GLA_DV = 256
GLA_RANK = 16
GLA_NORMALIZER = 16.0

RWKV_HEAD = 64
GROUP_HEADS = MXU_DIM_V7X // RWKV_HEAD
GROUP_LANES = GROUP_HEADS * RWKV_HEAD
RWKV_RANK = 64


def _dot(a, b):
    return jnp.dot(a, b, preferred_element_type=F32)


def _dot_nt(a, b):
    return lax.dot_general(a, b, (((1,), (1,)), ((), ())), preferred_element_type=F32)


def _dot_tn(a, b):
    return lax.dot_general(a, b, (((0,), (0,)), ((), ())), preferred_element_type=F32)


def _split3(x):
    x1 = x.astype(BF16)
    r1 = x - x1.astype(F32)
    x2 = r1.astype(BF16)
    r2 = r1 - x2.astype(F32)
    return x1, x2, r2.astype(BF16)


def _dot_01_lhs(m01, x):
    x1, x2, x3 = _split3(x)
    return _dot(m01, x1) + _dot(m01, x2) + _dot(m01, x3)


def _dot_01_rhs(x, m01):
    x1 = x.astype(BF16)
    x2 = (x - x1.astype(F32)).astype(BF16)
    return _dot(x1, m01) + _dot(x2, m01)


def _rmsnorm(x, g):
    ms = jnp.mean(x * x, axis=-1, keepdims=True)
    return x * lax.rsqrt(ms + RMS_EPS) * g


def _sigmoid(z):
    return 1.0 / (1.0 + jnp.exp(-z))


def _chunk_tri(tile):
    r = lax.broadcasted_iota(jnp.int32, (tile, tile), 0)
    c = lax.broadcasted_iota(jnp.int32, (tile, tile), 1)
    return ((r // CHUNK == c // CHUNK) & (c <= r)).astype(BF16)


def _chunk_total(cum, tile):
    nc = tile // CHUNK
    width = cum.shape[-1]
    c3 = cum.reshape(nc, CHUNK, width)
    last = c3[:, CHUNK - 1:CHUNK, :]
    return jnp.broadcast_to(last, (nc, CHUNK, width)).reshape(tile, width)


def _gla_kernel(x_ref, pre_g_ref, wq_ref, wk_ref, wv_ref, wg_ref, wad_ref, wup_ref, b_ref,
                ghead_ref, wout_ref, post_g_ref, o_ref, st_ref, o_acc_ref):
    tile = x_ref.shape[1]
    nc = tile // CHUNK

    @pl.when(pl.program_id(1) == 0)
    def _():
        st_ref[...] = jnp.zeros_like(st_ref)

    x = x_ref[0]
    h = _rmsnorm(x, pre_g_ref[...]).astype(BF16)
    q = _dot(h, wq_ref[...])
    k = _dot(h, wk_ref[...])
    v = _dot(h, wv_ref[...])
    gate = _dot(h, wg_ref[...])
    ad = _dot(h, wad_ref[...])
    z = _dot(ad.astype(BF16), wup_ref[...]) + b_ref[...]
    log_alpha = (jnp.minimum(z, 0.0) - jnp.log1p(jnp.exp(-jnp.abs(z)))) * (1.0 / GLA_NORMALIZER)

    bcum = _dot_01_lhs(_chunk_tri(tile), log_alpha)
    btot = _chunk_total(bcum, tile)
    q_dec = (q * (GLA_DK ** -0.5) * jnp.exp(bcum)).astype(BF16)
    k_dec = (k * jnp.exp(-bcum)).astype(BF16)
    k_end = (k * jnp.exp(btot - bcum)).astype(BF16)
    chunk_decay = jnp.exp(btot)
    vb = v.astype(BF16)

    row = lax.broadcasted_iota(jnp.int32, (CHUNK, CHUNK), 0)
    col = lax.broadcasted_iota(jnp.int32, (CHUNK, CHUNK), 1)
    causal = col <= row

    pairs = [(c, hd) for c in range(nc) for hd in range(GLA_HEADS)]
    rows_of = lambda c: slice(c * CHUNK, (c + 1) * CHUNK)
    kl_of = lambda hd: slice(hd * GLA_DK, (hd + 1) * GLA_DK)
    vl_of = lambda hd: slice(hd * GLA_DV, (hd + 1) * GLA_DV)
    qd = [q_dec[rows_of(c), kl_of(hd)] for c, hd in pairs]
    vc = [vb[rows_of(c), vl_of(hd)] for c, hd in pairs]
    att = [jnp.where(causal, _dot_nt(qd[i], k_dec[rows_of(c), kl_of(hd)]), 0.0).astype(BF16)
           for i, (c, hd) in enumerate(pairs)]
    kv = [_dot_tn(vc[i], k_end[rows_of(c), kl_of(hd)]) for i, (c, hd) in enumerate(pairs)]
    o_intra = [_dot(att[i], vc[i]) for i in range(len(pairs))]
    st_in = [None] * len(pairs)
    for hd in range(GLA_HEADS):
        st = st_ref[hd]
        for c in range(nc):
            i = c * GLA_HEADS + hd
            st_in[i] = st.astype(BF16)
            st = st * chunk_decay[c * CHUNK:c * CHUNK + 1, kl_of(hd)] + kv[i]
        st_ref[hd] = st
    for i, (c, hd) in enumerate(pairs):
        o_acc_ref[rows_of(c), vl_of(hd)] = o_intra[i] + _dot_nt(qd[i], st_in[i])

    o = o_acc_ref[...]
    outs = []
    for hd in range(GLA_HEADS):
        oh = o[:, hd * GLA_DV:(hd + 1) * GLA_DV]
        outs.append(oh * lax.rsqrt(jnp.mean(oh * oh, axis=-1, keepdims=True) + RMS_EPS) * ghead_ref[...])
    o = jnp.concatenate(outs, axis=-1)
    o = o * (gate * _sigmoid(gate))
    y = _dot(o.astype(BF16), wout_ref[...])
    o_ref[0] = x + _rmsnorm(y, post_g_ref[...])


def _const_spec(shape):
    nd = len(shape)
    return pl.BlockSpec(shape, lambda b, t: (0,) * nd)


def _gla_layer(x, pre_g, w_in, w_up, b_alpha, g_head, w_out, post_g):
    B, T, D = x.shape
    kw = GLA_HEADS * GLA_DK
    vw = GLA_HEADS * GLA_DV
    wq = w_in[:, :kw].astype(BF16)
    wk = w_in[:, kw:2 * kw].astype(BF16)
    wv = w_in[:, 2 * kw:2 * kw + vw].astype(BF16)
    wg = w_in[:, 2 * kw + vw:2 * kw + 2 * vw].astype(BF16)
    wad = jnp.pad(w_in[:, 2 * kw + 2 * vw:], ((0, 0), (0, LANES_V7X - GLA_RANK))).astype(BF16)
    wup = jnp.pad(w_up, ((0, LANES_V7X - GLA_RANK), (0, 0))).astype(BF16)
    args = (x, pre_g.reshape(1, D), wq, wk, wv, wg, wad, wup, b_alpha.reshape(1, kw),
            g_head.reshape(1, GLA_DV), w_out.astype(BF16), post_g.reshape(1, D))
    tile_spec = pl.BlockSpec((1, TILE_T, D), lambda b, t: (b, t, 0))
    in_specs = [tile_spec] + [_const_spec(a.shape) for a in args[1:]]
    return pl.pallas_call(
        _gla_kernel,
        grid=(B, T // TILE_T),
        in_specs=in_specs,
        out_specs=tile_spec,
        out_shape=jax.ShapeDtypeStruct((B, T, D), x.dtype),
        scratch_shapes=[pltpu.VMEM((GLA_HEADS, GLA_DV, GLA_DK), F32),
                        pltpu.VMEM((TILE_T, vw), F32)],
        compiler_params=pltpu.CompilerParams(
            dimension_semantics=("arbitrary", "arbitrary"),
            vmem_limit_bytes=VMEM_LIMIT_BYTES),
        name="gla_layer",
    )(*args)


def _block_rows(x, bd_mask):
    return jnp.where(bd_mask, jnp.concatenate([x] * GROUP_HEADS, axis=0), jnp.zeros((), x.dtype))


def _rwkv_kernel(x_ref, pre_g_ref, wmain_ref, wlo_ref, mu_main_ref, mu_lo_ref, wup_ref, w0_ref, a0_ref,
                 kk_ref, ka_ref, rk_ref, lnw_ref, lnb_ref, gdown_ref, gup_ref, wout_ref, post_g_ref,
                 o_ref,
                 ht_ref, prev_main_ref, prev_lo_ref,
                 at_ref, rt_ref, bt_ref, kt_ref, bh_ref, kh_ref, vb_ref, gam_ref, y_ref):
    tile = x_ref.shape[1]
    width = at_ref.shape[1]
    nc = tile // CHUNK
    ng = width // GROUP_LANES

    @pl.when(pl.program_id(1) == 0)
    def _():
        ht_ref[...] = jnp.zeros_like(ht_ref)
        prev_main_ref[...] = jnp.zeros_like(prev_main_ref)
        prev_lo_ref[...] = jnp.zeros_like(prev_lo_ref)

    x = x_ref[0]
    h = _rmsnorm(x, pre_g_ref[...]).astype(BF16)

    def shifted_lerp(p, prev_ref, mu):
        first = lax.broadcasted_iota(jnp.int32, p.shape, 0) == 0
        shifted = jnp.where(first, prev_ref[...], pltpu.roll(p, 1, 0))
        prev_ref[...] = p[tile - 1:tile, :]
        return p + mu * (shifted - p)

    p_main = shifted_lerp(_dot(h, wmain_ref[...]), prev_main_ref, mu_main_ref[...])
    p_lo = shifted_lerp(_dot(h, wlo_ref[...]), prev_lo_ref, mu_lo_ref[...])
    r = p_main[:, 0 * width:1 * width]
    k = p_main[:, 1 * width:2 * width]
    v = p_main[:, 2 * width:3 * width]
    gate = p_main[:, 3 * width:4 * width]

    lo_lane = lax.broadcasted_iota(jnp.int32, p_lo.shape, 1)
    lo_act = jnp.where(lo_lane < RWKV_RANK, jnp.tanh(p_lo), p_lo).astype(BF16)
    up = _dot(lo_act, wup_ref[...])
    logw = -math.exp(-0.5) * _sigmoid(w0_ref[...] + up[:, :width])
    a = _sigmoid(a0_ref[...] + up[:, width:])

    def head_sum(t):
        return _dot_01_rhs(_dot_01_rhs(t, gdown_ref[...]), gup_ref[...])

    kk = k * kk_ref[...]
    kk = kk / jnp.maximum(jnp.sqrt(head_sum(kk * kk)), 1e-12)
    k = k * (1.0 + (a - 1.0) * ka_ref[...])

    lcum = _dot_01_lhs(_chunk_tri(tile), logw)
    ltot = _chunk_total(lcum, tile)
    kka = kk * a
    e_neg = jnp.exp(-lcum)
    e_end = jnp.exp(ltot - lcum)
    at_ref[...] = (-kk * jnp.exp(lcum - logw)).astype(BF16)
    rt_ref[...] = (r * jnp.exp(lcum)).astype(BF16)
    bt_ref[...] = (kka * e_neg).astype(BF16)
    kt_ref[...] = (k * e_neg).astype(BF16)
    bh_ref[...] = (kka * e_end).astype(BF16)
    kh_ref[...] = (k * e_end).astype(BF16)
    vb_ref[...] = v.astype(BF16)
    gam_ref[...] = jnp.exp(ltot)

    bonus = head_sum(r * k * rk_ref[...]) * v

    br = lax.broadcasted_iota(jnp.int32, (GROUP_LANES, GROUP_LANES), 0)
    bc = lax.broadcasted_iota(jnp.int32, (GROUP_LANES, GROUP_LANES), 1)
    bd_mask = (br // RWKV_HEAD) == (bc // RWKV_HEAD)
    tr = lax.broadcasted_iota(jnp.int32, (CHUNK, GROUP_LANES), 0)
    ts = lax.broadcasted_iota(jnp.int32, (CHUNK, GROUP_LANES), 1) % CHUNK
    strict = ts < tr
    causal = ts <= tr
    eye = (ts == tr).astype(F32)

    cgs = [(c, g) for c in range(nc) for g in range(ng)]
    rows_of = lambda c: slice(c * CHUNK, (c + 1) * CHUNK)
    lanes_of = lambda g: slice(g * GROUP_LANES, (g + 1) * GROUP_LANES)
    blk = lambda t: _block_rows(t.astype(BF16), bd_mask)

    a_t = [at_ref[rows_of(c), lanes_of(g)] for c, g in cgs]
    r_t = [rt_ref[rows_of(c), lanes_of(g)] for c, g in cgs]
    vv = [vb_ref[rows_of(c), lanes_of(g)] for c, g in cgs]
    gram = [_dot_nt(jnp.concatenate([a_t[i], r_t[i]], axis=0),
                    jnp.concatenate([blk(bt_ref[rows_of(c), lanes_of(g)]),
                                     blk(kt_ref[rows_of(c), lanes_of(g)])], axis=0))
            for i, (c, g) in enumerate(cgs)]
    l_ab = [jnp.where(strict, m[:CHUNK, :GROUP_LANES], 0.0) for m in gram]
    l_ak = [jnp.where(strict, m[:CHUNK, GROUP_LANES:], 0.0).astype(BF16) for m in gram]
    m_rb = [jnp.where(causal, m[CHUNK:, :GROUP_LANES], 0.0).astype(BF16) for m in gram]
    m_rk = [jnp.where(causal, m[CHUNK:, GROUP_LANES:], 0.0).astype(BF16) for m in gram]

    t_inv = [eye + m for m in l_ab]
    pw = [_dot(m.astype(BF16), blk(m)) for m in l_ab]
    for _ in range(int(math.log2(CHUNK)) - 2):
        both = [_dot(jnp.concatenate([t, p], axis=0).astype(BF16), blk(p)) for t, p in zip(t_inv, pw)]
        t_inv = [t + m[:CHUNK] for t, m in zip(t_inv, both)]
        pw = [m[CHUNK:] for m in both]
    t_inv = [t + _dot(t.astype(BF16), blk(p)) for t, p in zip(t_inv, pw)]

    vblk = [blk(m) for m in vv]
    akv_yk = [_dot(jnp.concatenate([l_ak[i], m_rk[i]], axis=0), vblk[i]) for i in range(len(cgs))]
    wu = [_dot(t_inv[i].astype(BF16),
               jnp.concatenate([blk(a_t[i]), blk(akv_yk[i][:CHUNK])], axis=1)) for i in range(len(cgs))]

    for c in range(nc):
        idx = [c * ng + g for g in range(ng)]
        ht = [ht_ref[g] for g in range(ng)]
        wh = [_dot_nt(jnp.concatenate([wu[i][:, :GROUP_LANES].astype(BF16), r_t[i]], axis=0),
                      ht[g].astype(BF16)) for g, i in enumerate(idx)]
        u = [(wh[g][:CHUNK] + wu[i][:, GROUP_LANES:]).astype(BF16) for g, i in enumerate(idx)]
        upd = [_dot_tn(jnp.concatenate([u[g], vv[i]], axis=0),
                       jnp.concatenate([bh_ref[rows_of(c), lanes_of(g)], kh_ref[rows_of(c), lanes_of(g)]], axis=0))
               for g, i in enumerate(idx)]
        for g, i in enumerate(idx):
            ht_ref[g] = ht[g] * gam_ref[c * CHUNK:c * CHUNK + 1, lanes_of(g)] + jnp.where(bd_mask, upd[g], 0.0)
            y_ref[rows_of(c), lanes_of(g)] = wh[g][CHUNK:] + akv_yk[i][CHUNK:] + _dot(m_rb[i], blk(u[g]))

    y = y_ref[...]
    mean = head_sum(y) * (1.0 / RWKV_HEAD)
    yc = y - mean
    var = head_sum(yc * yc) * (1.0 / RWKV_HEAD)
    y = yc * lax.rsqrt(var + GN_EPS) * lnw_ref[...] + lnb_ref[...]
    y = (y + bonus) * (gate * _sigmoid(gate))
    out = _dot(y.astype(BF16), wout_ref[...])
    o_ref[0] = x + _rmsnorm(out, post_g_ref[...])


def _rwkv_layer(x, pre_g, w_in, mu, w0, w_decay_up, a0, w_iclr_up, k_k, k_a, r_k, ln_w, ln_b, w_out, post_g):
    B, T, D = x.shape
    width = w_out.shape[0]
    heads = width // RWKV_HEAD
    rank2 = 2 * RWKV_RANK
    wmain = w_in[:, :4 * width].astype(BF16)
    wlo = w_in[:, 4 * width:].astype(BF16)
    zeros = jnp.zeros((RWKV_RANK, width), F32)
    wup = jnp.concatenate([jnp.concatenate([w_decay_up, zeros], axis=1),
                           jnp.concatenate([zeros, w_iclr_up], axis=1)], axis=0).astype(BF16)
    head_of_lane = jnp.arange(width) // RWKV_HEAD
    gdown = (head_of_lane[:, None] == jnp.arange(LANES_V7X)[None, :]).astype(BF16)
    gup = gdown.T
    row = lambda t: t.reshape(1, -1)
    args = (x, row(pre_g), wmain, wlo, row(mu[:4 * width]), row(mu[4 * width:]), wup, row(w0), row(a0),
            row(k_k), row(k_a), row(r_k), row(ln_w), row(ln_b), gdown, gup, w_out.astype(BF16), row(post_g))
    tile_spec = pl.BlockSpec((1, TILE_T, D), lambda b, t: (b, t, 0))
    in_specs = [tile_spec] + [_const_spec(a.shape) for a in args[1:]]
    ngroups = width // GROUP_LANES
    scratch = [pltpu.VMEM((ngroups, GROUP_LANES, GROUP_LANES), F32),
               pltpu.VMEM((1, 4 * width), F32),
               pltpu.VMEM((1, rank2), F32)]
    scratch += [pltpu.VMEM((TILE_T, width), BF16)] * 7
    scratch += [pltpu.VMEM((TILE_T, width), F32)] * 2
    assert heads * RWKV_HEAD == width and rank2 == LANES_V7X
    return pl.pallas_call(
        _rwkv_kernel,
        grid=(B, T // TILE_T),
        in_specs=in_specs,
        out_specs=tile_spec,
        out_shape=jax.ShapeDtypeStruct((B, T, D), x.dtype),
        scratch_shapes=scratch,
        compiler_params=pltpu.CompilerParams(
            dimension_semantics=("arbitrary", "arbitrary"),
            vmem_limit_bytes=VMEM_LIMIT_BYTES),
        name="rwkv_layer",
    )(*args)


def kernel(x, gla_pre_norm, gla_w_in, gla_w_alpha_up, gla_b_alpha, gla_head_norm, gla_w_out, gla_post_norm,
           rwkv_pre_norm, rwkv_w_in, rwkv_mu, rwkv_w0, rwkv_w_decay_up, rwkv_a0, rwkv_w_iclr_up, rwkv_k_k,
           rwkv_k_a, rwkv_r_k, rwkv_ln_w, rwkv_ln_b, rwkv_w_out, rwkv_post_norm):
    depth = gla_pre_norm.shape[0] + rwkv_pre_norm.shape[0]
    for i in range(depth):
        j = i // 2
        if i % 2 == 0:
            x = _gla_layer(x, gla_pre_norm[j], gla_w_in[j], gla_w_alpha_up[j], gla_b_alpha[j],
                           gla_head_norm[j], gla_w_out[j], gla_post_norm[j])
        else:
            x = _rwkv_layer(x, rwkv_pre_norm[j], rwkv_w_in[j], rwkv_mu[j], rwkv_w0[j], rwkv_w_decay_up[j],
                            rwkv_a0[j], rwkv_w_iclr_up[j], rwkv_k_k[j], rwkv_k_a[j], rwkv_r_k[j].reshape(-1),
                            rwkv_ln_w[j], rwkv_ln_b[j], rwkv_w_out[j], rwkv_post_norm[j])
    return x
```

```python
import math

import jax
import jax.numpy as jnp
from jax import lax
from jax.experimental import pallas as pl
from jax.experimental.pallas import tpu as pltpu

F32 = jnp.float32
BF16 = jnp.bfloat16

LANES_V7X = 128
MXU_DIM_V7X = 256
VMEM_LIMIT_BYTES = 56 * 1024 * 1024

RMS_EPS = 1e-6
GN_EPS = 64e-5

CHUNK = 64
TILE_T = 256

GLA_HEADS = 4
GLA_DK = 128
GLA_DV = 256
GLA_RANK = 16
GLA_NORMALIZER = 16.0

RWKV_HEAD = 64
GROUP_HEADS = MXU_DIM_V7X // RWKV_HEAD
GROUP_LANES = GROUP_HEADS * RWKV_HEAD
RWKV_RANK = 64


def _dot(a, b):
    return jnp.dot(a, b, preferred_element_type=F32)


def _dot_nt(a, b):
    return lax.dot_general(a, b, (((1,), (1,)), ((), ())), preferred_element_type=F32)


def _dot_tn(a, b):
    return lax.dot_general(a, b, (((0,), (0,)), ((), ())), preferred_element_type=F32)


def _split2(x):
    x1 = x.astype(BF16)
    return x1, (x - x1.astype(F32)).astype(BF16)


def _dot_01_lhs(m01, x):
    x1, x2 = _split2(x)
    return _dot(m01, x1) + _dot(m01, x2)


def _dot_01_rhs(x, m01, terms):
    if terms == 1:
        return _dot(x.astype(BF16), m01)
    x1, x2 = _split2(x)
    return _dot(x1, m01) + _dot(x2, m01)


def _rmsnorm(x, g):
    ms = jnp.mean(x * x, axis=-1, keepdims=True)
    return x * lax.rsqrt(ms + RMS_EPS) * g


def _sigmoid(z):
    return 1.0 / (1.0 + jnp.exp(-z))


def _chunk_tri(tile):
    r = lax.broadcasted_iota(jnp.int32, (tile, tile), 0)
    c = lax.broadcasted_iota(jnp.int32, (tile, tile), 1)
    return ((r // CHUNK == c // CHUNK) & (c <= r)).astype(BF16)


def _chunk_total(cum, tile):
    nc = tile // CHUNK
    width = cum.shape[-1]
    c3 = cum.reshape(nc, CHUNK, width)
    last = c3[:, CHUNK - 1:CHUNK, :]
    return jnp.broadcast_to(last, (nc, CHUNK, width)).reshape(tile, width)


def _rows_of(c):
    return slice(c * CHUNK, (c + 1) * CHUNK)


def _interleave(gens):
    gens = list(gens)
    while gens:
        alive = []
        for g in gens:
            try:
                next(g)
                alive.append(g)
            except StopIteration:
                pass
        gens = alive


def _software_pipeline(nb, phases):
    for step in range(nb + len(phases) - 1):
        _interleave(phases[k](step - k) for k in reversed(range(len(phases))) if 0 <= step - k < nb)


def _gla_kernel(x_ref, pre_g_ref, wq_ref, wk_ref, wv_ref, wg_ref, wad_ref, wup_ref, b_ref,
                ghead_ref, wout_ref, post_g_ref, o_ref, st_ref, o_acc_ref):
    nb, tile, _ = x_ref.shape
    nc = tile // CHUNK

    @pl.when(pl.program_id(0) == 0)
    def _():
        st_ref[...] = jnp.zeros_like(st_ref)

    tri = _chunk_tri(tile)
    row = lax.broadcasted_iota(jnp.int32, (CHUNK, CHUNK), 0)
    col = lax.broadcasted_iota(jnp.int32, (CHUNK, CHUNK), 1)
    causal = col <= row
    pairs = [(c, hd) for c in range(nc) for hd in range(GLA_HEADS)]
    kl_of = lambda hd: slice(hd * GLA_DK, (hd + 1) * GLA_DK)
    vl_of = lambda hd: slice(hd * GLA_DV, (hd + 1) * GLA_DV)

    def project(b, p):
        h = _rmsnorm(x_ref[b], pre_g_ref[...]).astype(BF16)
        for name, w_ref in (("ad", wad_ref), ("k", wk_ref), ("q", wq_ref), ("v", wv_ref), ("gate", wg_ref)):
            p[name] = _dot(h, w_ref[...])
            yield

    def decays(b, p, d):
        z = _dot(p["ad"].astype(BF16), wup_ref[...]) + b_ref[...]
        yield
        log_alpha = (jnp.minimum(z, 0.0) - jnp.log1p(jnp.exp(-jnp.abs(z)))) * (1.0 / GLA_NORMALIZER)
        bcum = _dot_01_lhs(tri, log_alpha)
        yield
        btot = _chunk_total(bcum, tile)
        d.update(q_dec=(p["q"] * (GLA_DK ** -0.5) * jnp.exp(bcum)).astype(BF16),
                 k_dec=(p["k"] * jnp.exp(-bcum)).astype(BF16),
                 k_end=(p["k"] * jnp.exp(btot - bcum)).astype(BF16),
                 chunk_decay=jnp.exp(btot), vb=p["v"].astype(BF16))

    def chunks(b, d):
        qd = [d["q_dec"][_rows_of(c), kl_of(hd)] for c, hd in pairs]
        vc = [d["vb"][_rows_of(c), vl_of(hd)] for c, hd in pairs]
        att = [jnp.where(causal, _dot_nt(qd[i], d["k_dec"][_rows_of(c), kl_of(hd)]), 0.0).astype(BF16)
               for i, (c, hd) in enumerate(pairs)]
        yield
        kv = [_dot_tn(vc[i], d["k_end"][_rows_of(c), kl_of(hd)]) for i, (c, hd) in enumerate(pairs)]
        yield
        o_intra = [_dot(att[i], vc[i]) for i in range(len(pairs))]
        yield
        st_in = [None] * len(pairs)
        for hd in range(GLA_HEADS):
            st = st_ref[b * GLA_HEADS + hd]
            for c in range(nc):
                i = c * GLA_HEADS + hd
                st_in[i] = st.astype(BF16)
                st = st * d["chunk_decay"][c * CHUNK:c * CHUNK + 1, kl_of(hd)] + kv[i]
            st_ref[b * GLA_HEADS + hd] = st
        for i, (c, hd) in enumerate(pairs):
            o_acc_ref[b, _rows_of(c), vl_of(hd)] = o_intra[i] + _dot_nt(qd[i], st_in[i])

    def output(b, p):
        o = o_acc_ref[b]
        outs = []
        for hd in range(GLA_HEADS):
            oh = o[:, vl_of(hd)]
            outs.append(oh * lax.rsqrt(jnp.mean(oh * oh, axis=-1, keepdims=True) + RMS_EPS) * ghead_ref[...])
        o = jnp.concatenate(outs, axis=-1) * (p["gate"] * _sigmoid(p["gate"]))
        y = _dot(o.astype(BF16), wout_ref[...])
        yield
        o_ref[b] = x_ref[b] + _rmsnorm(y, post_g_ref[...])

    proj = [dict() for _ in range(nb)]
    dec = [dict() for _ in range(nb)]
    _software_pipeline(nb, [lambda b: project(b, proj[b]),
                            lambda b: decays(b, proj[b], dec[b]),
                            lambda b: chunks(b, dec[b])])
    _interleave([output(b, proj[b]) for b in range(nb)])


def _const_spec(shape):
    nd = len(shape)
    return pl.BlockSpec(shape, lambda t: (0,) * nd, pipeline_mode=pl.Buffered(1))


def _gla_layer(x, pre_g, w_in, w_up, b_alpha, g_head, w_out, post_g):
    B, T, D = x.shape
    kw = GLA_HEADS * GLA_DK
    vw = GLA_HEADS * GLA_DV
    wq = w_in[:, :kw].astype(BF16)
    wk = w_in[:, kw:2 * kw].astype(BF16)
    wv = w_in[:, 2 * kw:2 * kw + vw].astype(BF16)
    wg = w_in[:, 2 * kw + vw:2 * kw + 2 * vw].astype(BF16)
    wad = jnp.pad(w_in[:, 2 * kw + 2 * vw:], ((0, 0), (0, LANES_V7X - GLA_RANK))).astype(BF16)
    wup = jnp.pad(w_up, ((0, LANES_V7X - GLA_RANK), (0, 0))).astype(BF16)
    args = (x, pre_g.reshape(1, D), wq, wk, wv, wg, wad, wup, b_alpha.reshape(1, kw),
            g_head.reshape(1, GLA_DV), w_out.astype(BF16), post_g.reshape(1, D))
    tile_spec = pl.BlockSpec((B, TILE_T, D), lambda t: (0, t, 0))
    in_specs = [tile_spec] + [_const_spec(a.shape) for a in args[1:]]
    return pl.pallas_call(
        _gla_kernel,
        grid=(T // TILE_T,),
        in_specs=in_specs,
        out_specs=tile_spec,
        out_shape=jax.ShapeDtypeStruct((B, T, D), x.dtype),
        scratch_shapes=[pltpu.VMEM((B * GLA_HEADS, GLA_DV, GLA_DK), F32),
                        pltpu.VMEM((B, TILE_T, vw), F32)],
        compiler_params=pltpu.CompilerParams(
            dimension_semantics=("arbitrary",),
            vmem_limit_bytes=VMEM_LIMIT_BYTES),
        name="gla_layer",
    )(*args)


def _block_rows(x, bd_mask):
    return jnp.where(bd_mask, jnp.concatenate([x] * GROUP_HEADS, axis=0), jnp.zeros((), x.dtype))


def _rwkv_kernel(x_ref, pre_g_ref, wmain_ref, wlo_ref, mu_main_ref, mu_lo_ref, wup_ref, w0_ref, a0_ref,
                 kk_ref, ka_ref, rk_ref, lnw_ref, lnb_ref, gdown_ref, gup_ref, wout_ref, post_g_ref,
                 o_ref,
                 ht_ref, prev_main_ref, prev_lo_ref,
                 at_ref, rt_ref, bt_ref, kt_ref, bh_ref, kh_ref, vb_ref, gam_ref, y_ref):
    nb, tile, _ = x_ref.shape
    width = at_ref.shape[2]
    nc = tile // CHUNK
    ng = width // GROUP_LANES

    @pl.when(pl.program_id(0) == 0)
    def _():
        ht_ref[...] = jnp.zeros_like(ht_ref)
        prev_main_ref[...] = jnp.zeros_like(prev_main_ref)
        prev_lo_ref[...] = jnp.zeros_like(prev_lo_ref)

    tri = _chunk_tri(tile)
    br = lax.broadcasted_iota(jnp.int32, (GROUP_LANES, GROUP_LANES), 0)
    bc = lax.broadcasted_iota(jnp.int32, (GROUP_LANES, GROUP_LANES), 1)
    bd_mask = (br // RWKV_HEAD) == (bc // RWKV_HEAD)
    tr = lax.broadcasted_iota(jnp.int32, (CHUNK, GROUP_LANES), 0)
    ts = lax.broadcasted_iota(jnp.int32, (CHUNK, GROUP_LANES), 1) % CHUNK
    strict = ts < tr
    causal = ts <= tr
    eye = (ts == tr).astype(F32)
    cgs = [(c, g) for c in range(nc) for g in range(ng)]
    lanes_of = lambda g: slice(g * GROUP_LANES, (g + 1) * GROUP_LANES)
    blk = lambda t: _block_rows(t.astype(BF16), bd_mask)

    def head_sum(t, up_terms=1):
        return _dot_01_rhs(_dot_01_rhs(t, gdown_ref[...], 1), gup_ref[...], up_terms)

    main_cols = dict(r=0, k=1, v=2, gate=3)

    def project(b, p):
        h = _rmsnorm(x_ref[b], pre_g_ref[...]).astype(BF16)
        p["lo"] = _dot(h, wlo_ref[...])
        yield
        for name in ("k", "r", "v", "gate"):
            j = main_cols[name]
            p[name] = _dot(h, wmain_ref[:, j * width:(j + 1) * width])
            yield

    def shifted_lerp(p, prev, mu):
        first = lax.broadcasted_iota(jnp.int32, p.shape, 0) == 0
        shifted = jnp.where(first, prev[...], pltpu.roll(p, 1, 0))
        prev[...] = p[tile - 1:tile, :]
        return p + mu * (shifted - p)

    def lerp_main(b, p, name):
        cols = slice(main_cols[name] * width, (main_cols[name] + 1) * width)
        return shifted_lerp(p[name], prev_main_ref.at[b, :, cols], mu_main_ref[:, cols])

    def elementwise(b, p, e):
        p_lo = shifted_lerp(p["lo"], prev_lo_ref.at[b], mu_lo_ref[...])
        lo_lane = lax.broadcasted_iota(jnp.int32, p_lo.shape, 1)
        lo_act = jnp.where(lo_lane < RWKV_RANK, jnp.tanh(p_lo), p_lo).astype(BF16)
        up = _dot(lo_act, wup_ref[...])
        yield
        logw = -math.exp(-0.5) * _sigmoid(w0_ref[...] + up[:, :width])
        a = _sigmoid(a0_ref[...] + up[:, width:])
        k = lerp_main(b, p, "k")
        kk = k * kk_ref[...]
        kk_sq = head_sum(kk * kk)
        yield
        kk = kk / jnp.maximum(jnp.sqrt(kk_sq), 1e-12)
        k = k * (1.0 + (a - 1.0) * ka_ref[...])
        lcum = _dot_01_lhs(tri, logw)
        yield
        r = lerp_main(b, p, "r")
        v = lerp_main(b, p, "v")
        e["bonus"] = head_sum(r * k * rk_ref[...]) * v
        yield
        ltot = _chunk_total(lcum, tile)
        kka = kk * a
        e_neg = jnp.exp(-lcum)
        e_end = jnp.exp(ltot - lcum)
        at_ref[b] = (-kk * jnp.exp(lcum - logw)).astype(BF16)
        rt_ref[b] = (r * jnp.exp(lcum)).astype(BF16)
        bt_ref[b] = (kka * e_neg).astype(BF16)
        kt_ref[b] = (k * e_neg).astype(BF16)
        bh_ref[b] = (kka * e_end).astype(BF16)
        kh_ref[b] = (k * e_end).astype(BF16)
        vb_ref[b] = v.astype(BF16)
        gam_ref[b] = jnp.exp(ltot)
        e["gate"] = lerp_main(b, p, "gate")

    def chunk_algebra(b, alg):
        a_t = [at_ref[b, _rows_of(c), lanes_of(g)] for c, g in cgs]
        r_t = [rt_ref[b, _rows_of(c), lanes_of(g)] for c, g in cgs]
        vv = [vb_ref[b, _rows_of(c), lanes_of(g)] for c, g in cgs]
        gram = [_dot_nt(jnp.concatenate([a_t[i], r_t[i]], axis=0),
                        jnp.concatenate([blk(bt_ref[b, _rows_of(c), lanes_of(g)]),
                                         blk(kt_ref[b, _rows_of(c), lanes_of(g)])], axis=0))
                for i, (c, g) in enumerate(cgs)]
        yield
        l_ab = [jnp.where(strict, m[:CHUNK, :GROUP_LANES], 0.0) for m in gram]
        l_ak = [jnp.where(strict, m[:CHUNK, GROUP_LANES:], 0.0).astype(BF16) for m in gram]
        m_rb = [jnp.where(causal, m[CHUNK:, :GROUP_LANES], 0.0).astype(BF16) for m in gram]
        m_rk = [jnp.where(causal, m[CHUNK:, GROUP_LANES:], 0.0).astype(BF16) for m in gram]

        t_inv = [eye + m for m in l_ab]
        pw = [_dot(m.astype(BF16), blk(m)) for m in l_ab]
        yield
        for _ in range(int(math.log2(CHUNK)) - 2):
            both = [_dot(jnp.concatenate([t, p], axis=0).astype(BF16), blk(p)) for t, p in zip(t_inv, pw)]
            yield
            t_inv = [t + m[:CHUNK] for t, m in zip(t_inv, both)]
            pw = [m[CHUNK:] for m in both]
        t_inv = [t + _dot(t.astype(BF16), blk(p)) for t, p in zip(t_inv, pw)]
        yield

        akv_yk = [_dot(jnp.concatenate([l_ak[i], m_rk[i]], axis=0), blk(vv[i])) for i in range(len(cgs))]
        yield
        wu = [_dot(t_inv[i].astype(BF16),
                   jnp.concatenate([blk(a_t[i]), blk(akv_yk[i][:CHUNK])], axis=1)) for i in range(len(cgs))]
        alg.update(r_t=r_t, vv=vv, m_rb=m_rb, y_k=[m[CHUNK:] for m in akv_yk],
                   w=[m[:, :GROUP_LANES].astype(BF16) for m in wu], u0=[m[:, GROUP_LANES:] for m in wu])

    def state_scan(alg):
        chains = [(b, g) for b in range(nb) for g in range(ng)]
        for c in range(nc):
            ht = [ht_ref[b * ng + g] for b, g in chains]
            wh = [_dot_nt(jnp.concatenate([alg[b]["w"][c * ng + g], alg[b]["r_t"][c * ng + g]], axis=0),
                          ht[j].astype(BF16)) for j, (b, g) in enumerate(chains)]
            u = [(wh[j][:CHUNK] + alg[b]["u0"][c * ng + g]).astype(BF16) for j, (b, g) in enumerate(chains)]
            upd = [_dot_tn(jnp.concatenate([u[j], alg[b]["vv"][c * ng + g]], axis=0),
                           jnp.concatenate([bh_ref[b, _rows_of(c), lanes_of(g)],
                                            kh_ref[b, _rows_of(c), lanes_of(g)]], axis=0))
                   for j, (b, g) in enumerate(chains)]
            for j, (b, g) in enumerate(chains):
                i = c * ng + g
                ht_ref[b * ng + g] = (ht[j] * gam_ref[b, c * CHUNK:c * CHUNK + 1, lanes_of(g)]
                                      + jnp.where(bd_mask, upd[j], 0.0))
                y_ref[b, _rows_of(c), lanes_of(g)] = (wh[j][CHUNK:] + alg[b]["y_k"][i]
                                                      + _dot(alg[b]["m_rb"][i], blk(u[j])))

    def output(b, e):
        y = y_ref[b]
        mean = head_sum(y, up_terms=2) * (1.0 / RWKV_HEAD)
        yield
        yc = y - mean
        var = head_sum(yc * yc) * (1.0 / RWKV_HEAD)
        yield
        y = yc * lax.rsqrt(var + GN_EPS) * lnw_ref[...] + lnb_ref[...]
        y = (y + e["bonus"]) * (e["gate"] * _sigmoid(e["gate"]))
        out = _dot(y.astype(BF16), wout_ref[...])
        yield
        o_ref[b] = x_ref[b] + _rmsnorm(out, post_g_ref[...])

    proj = [dict() for _ in range(nb)]
    elem = [dict() for _ in range(nb)]
    alg = [dict() for _ in range(nb)]
    _software_pipeline(nb, [lambda b: project(b, proj[b]),
                            lambda b: elementwise(b, proj[b], elem[b]),
                            lambda b: chunk_algebra(b, alg[b])])
    state_scan(alg)
    _interleave([output(b, elem[b]) for b in range(nb)])


def _rwkv_layer(x, pre_g, w_in, mu, w0, w_decay_up, a0, w_iclr_up, k_k, k_a, r_k, ln_w, ln_b, w_out, post_g):
    B, T, D = x.shape
    width = w_out.shape[0]
    heads = width // RWKV_HEAD
    rank2 = 2 * RWKV_RANK
    wmain = w_in[:, :4 * width].astype(BF16)
    wlo = w_in[:, 4 * width:].astype(BF16)
    zeros = jnp.zeros((RWKV_RANK, width), F32)
    wup = jnp.concatenate([jnp.concatenate([w_decay_up, zeros], axis=1),
                           jnp.concatenate([zeros, w_iclr_up], axis=1)], axis=0).astype(BF16)
    head_of_lane = jnp.arange(width) // RWKV_HEAD
    gdown = (head_of_lane[:, None] == jnp.arange(LANES_V7X)[None, :]).astype(BF16)
    gup = gdown.T
    row = lambda t: t.reshape(1, -1)
    args = (x, row(pre_g), wmain, wlo, row(mu[:4 * width]), row(mu[4 * width:]), wup, row(w0), row(a0),
            row(k_k), row(k_a), row(r_k), row(ln_w), row(ln_b), gdown, gup, w_out.astype(BF16), row(post_g))
    tile_spec = pl.BlockSpec((B, TILE_T, D), lambda t: (0, t, 0))
    in_specs = [tile_spec] + [_const_spec(a.shape) for a in args[1:]]
    ngroups = width // GROUP_LANES
    scratch = [pltpu.VMEM((B * ngroups, GROUP_LANES, GROUP_LANES), F32),
               pltpu.VMEM((B, 1, 4 * width), F32),
               pltpu.VMEM((B, 1, rank2), F32)]
    scratch += [pltpu.VMEM((B, TILE_T, width), BF16)] * 7
    scratch += [pltpu.VMEM((B, TILE_T, width), F32)] * 2
    assert heads * RWKV_HEAD == width and rank2 == LANES_V7X
    return pl.pallas_call(
        _rwkv_kernel,
        grid=(T // TILE_T,),
        in_specs=in_specs,
        out_specs=tile_spec,
        out_shape=jax.ShapeDtypeStruct((B, T, D), x.dtype),
        scratch_shapes=scratch,
        compiler_params=pltpu.CompilerParams(
            dimension_semantics=("arbitrary",),
            vmem_limit_bytes=VMEM_LIMIT_BYTES),
        name="rwkv_layer",
    )(*args)


def kernel(x, gla_pre_norm, gla_w_in, gla_w_alpha_up, gla_b_alpha, gla_head_norm, gla_w_out, gla_post_norm,
           rwkv_pre_norm, rwkv_w_in, rwkv_mu, rwkv_w0, rwkv_w_decay_up, rwkv_a0, rwkv_w_iclr_up, rwkv_k_k,
           rwkv_k_a, rwkv_r_k, rwkv_ln_w, rwkv_ln_b, rwkv_w_out, rwkv_post_norm):
    depth = gla_pre_norm.shape[0] + rwkv_pre_norm.shape[0]
    for i in range(depth):
        j = i // 2
        if i % 2 == 0:
            x = _gla_layer(x, gla_pre_norm[j], gla_w_in[j], gla_w_alpha_up[j], gla_b_alpha[j],
                           gla_head_norm[j], gla_w_out[j], gla_post_norm[j])
        else:
            x = _rwkv_layer(x, rwkv_pre_norm[j], rwkv_w_in[j], rwkv_mu[j], rwkv_w0[j], rwkv_w_decay_up[j],
                            rwkv_a0[j], rwkv_w_iclr_up[j], rwkv_k_k[j], rwkv_k_a[j], rwkv_r_k[j].reshape(-1),
                            rwkv_ln_w[j], rwkv_ln_b[j], rwkv_w_out[j], rwkv_post_norm[j])
    return x
```

```python
import math

import jax
import jax.numpy as jnp
from jax import lax
from jax.experimental import pallas as pl
from jax.experimental.pallas import tpu as pltpu

F32 = jnp.float32
BF16 = jnp.bfloat16

LANES_V7X = 128
MXU_DIM_V7X = 256
VMEM_LIMIT_BYTES = 56 * 1024 * 1024

RMS_EPS = 1e-6
GN_EPS = 64e-5

CHUNK = 64
TILE_T = 256

GLA_HEADS = 4
GLA_DK = 128
GLA_DV = 256
GLA_RANK = 16
GLA_NORMALIZER = 16.0

RWKV_HEAD = 64
GROUP_HEADS = MXU_DIM_V7X // RWKV_HEAD
GROUP_LANES = GROUP_HEADS * RWKV_HEAD
RWKV_RANK = 64
CHAINS_PER_STAGE = 16


def _dot(a, b):
    return jnp.dot(a, b, preferred_element_type=F32)


def _dot_nt(a, b):
    return lax.dot_general(a, b, (((1,), (1,)), ((), ())), preferred_element_type=F32)


def _dot_tn(a, b):
    return lax.dot_general(a, b, (((0,), (0,)), ((), ())), preferred_element_type=F32)


def _split2(x):
    x1 = x.astype(BF16)
    return x1, (x - x1.astype(F32)).astype(BF16)


def _dot_01_lhs(m01, x):
    x1, x2 = _split2(x)
    return _dot(m01, x1) + _dot(m01, x2)


def _dot_01_rhs(x, m01, terms):
    if terms == 1:
        return _dot(x.astype(BF16), m01)
    x1, x2 = _split2(x)
    return _dot(x1, m01) + _dot(x2, m01)


def _rmsnorm(x, g):
    ms = jnp.mean(x * x, axis=-1, keepdims=True)
    return x * lax.rsqrt(ms + RMS_EPS) * g


def _sigmoid(z):
    return 1.0 / (1.0 + jnp.exp(-z))


def _chunk_tri(tile):
    r = lax.broadcasted_iota(jnp.int32, (tile, tile), 0)
    c = lax.broadcasted_iota(jnp.int32, (tile, tile), 1)
    return ((r // CHUNK == c // CHUNK) & (c <= r)).astype(BF16)


def _chunk_total(cum, tile):
    nc = tile // CHUNK
    width = cum.shape[-1]
    c3 = cum.reshape(nc, CHUNK, width)
    last = c3[:, CHUNK - 1:CHUNK, :]
    return jnp.broadcast_to(last, (nc, CHUNK, width)).reshape(tile, width)


def _rows_of(c):
    return slice(c * CHUNK, (c + 1) * CHUNK)


def _interleave(gens):
    gens = list(gens)
    while gens:
        alive = []
        for g in gens:
            try:
                next(g)
                alive.append(g)
            except StopIteration:
                pass
        gens = alive


def _software_pipeline(nb, phases):
    for step in range(nb + len(phases) - 1):
        _interleave(phases[k](step - k) for k in reversed(range(len(phases))) if 0 <= step - k < nb)


def _gla_kernel(x_ref, pre_g_ref, win_ref, wup_ref, b_ref,
                ghead_ref, wout_ref, post_g_ref, o_ref, st_ref, o_acc_ref):
    nb, tile, _ = x_ref.shape
    nc = tile // CHUNK

    @pl.when(pl.program_id(0) == 0)
    def _():
        st_ref[...] = jnp.zeros_like(st_ref)

    tri = _chunk_tri(tile)
    row = lax.broadcasted_iota(jnp.int32, (CHUNK, CHUNK), 0)
    col = lax.broadcasted_iota(jnp.int32, (CHUNK, CHUNK), 1)
    causal = col <= row
    pairs =[(c, hd) for c in range(nc) for hd in range(GLA_HEADS)]
    kl_of = lambda hd: slice(hd * GLA_DK, (hd + 1) * GLA_DK)
    vl_of = lambda hd: slice(hd * GLA_DV, (hd + 1) * GLA_DV)

    def project(b, p):
        h = _rmsnorm(x_ref[b], pre_g_ref[...]).astype(BF16)
        kw, vw = GLA_HEADS * GLA_DK, GLA_HEADS * GLA_DV
        cols = dict(q=(0, kw), k=(kw, 2 * kw), v=(2 * kw, 2 * kw + vw), gate=(2 * kw + vw, 2 * kw + 2 * vw),
                    ad=(2 * kw + 2 * vw, win_ref.shape[1]))
        for name in ("ad", "k", "q", "v", "gate"):
            p[name] = _dot(h, win_ref[:, cols[name][0]:cols[name][1]])
            yield

    def decays(b, p, d):
        z = _dot(p["ad"].astype(BF16), wup_ref[...]) + b_ref[...]
        yield
        log_alpha = (jnp.minimum(z, 0.0) - jnp.log1p(jnp.exp(-jnp.abs(z)))) * (1.0 / GLA_NORMALIZER)
        bcum = _dot_01_lhs(tri, log_alpha)
        yield
        btot = _chunk_total(bcum, tile)
        d.update(q_dec=(p["q"] * (GLA_DK ** -0.5) * jnp.exp(bcum)).astype(BF16),
                 k_dec=(p["k"] * jnp.exp(-bcum)).astype(BF16),
                 k_end=(p["k"] * jnp.exp(btot - bcum)).astype(BF16),
                 chunk_decay=jnp.exp(btot), vb=p["v"].astype(BF16))

    def chunks(b, d):
        qd = [d["q_dec"][_rows_of(c), kl_of(hd)] for c, hd in pairs]
        vc = [d["vb"][_rows_of(c), vl_of(hd)] for c, hd in pairs]
        att = [jnp.where(causal, _dot_nt(qd[i], d["k_dec"][_rows_of(c), kl_of(hd)]), 0.0).astype(BF16)
               for i, (c, hd) in enumerate(pairs)]
        yield
        kv = [_dot_tn(vc[i], d["k_end"][_rows_of(c), kl_of(hd)]) for i, (c, hd) in enumerate(pairs)]
        yield
        o_intra = [_dot(att[i], vc[i]) for i in range(len(pairs))]
        yield
        st_in = [None] * len(pairs)
        for hd in range(GLA_HEADS):
            st = st_ref[b * GLA_HEADS + hd]
            for c in range(nc):
                i = c * GLA_HEADS + hd
                st_in[i] = st.astype(BF16)
                st = st * d["chunk_decay"][c * CHUNK:c * CHUNK + 1, kl_of(hd)] + kv[i]
            st_ref[b * GLA_HEADS + hd] = st
        for i, (c, hd) in enumerate(pairs):
            o_acc_ref[b, _rows_of(c), vl_of(hd)] = o_intra[i] + _dot_nt(qd[i], st_in[i])

    def output(b, p):
        o = o_acc_ref[b]
        outs = []
        for hd in range(GLA_HEADS):
            oh = o[:, vl_of(hd)]
            outs.append(oh * lax.rsqrt(jnp.mean(oh * oh, axis=-1, keepdims=True) + RMS_EPS) * ghead_ref[...])
        o = jnp.concatenate(outs, axis=-1) * (p["gate"] * _sigmoid(p["gate"]))
        y = _dot(o.astype(BF16), wout_ref[...])
        yield
        o_ref[b] = x_ref[b] + _rmsnorm(y, post_g_ref[...])

    proj = [dict() for _ in range(nb)]
    dec = [dict() for _ in range(nb)]
    _software_pipeline(nb, [lambda b: project(b, proj[b]),
                            lambda b: decays(b, proj[b], dec[b]),
                            lambda b: chunks(b, dec[b])])
    _interleave([output(b, proj[b]) for b in range(nb)])


def _const_spec(shape):
    nd = len(shape)
    return pl.BlockSpec(shape, lambda t: (0,) * nd, pipeline_mode=pl.Buffered(1))


def _gla_layer(x, pre_g, w_in, w_up, b_alpha, g_head, w_out, post_g):
    B, T, D = x.shape
    kw = GLA_HEADS * GLA_DK
    vw = GLA_HEADS * GLA_DV
    win = jnp.pad(w_in, ((0, 0), (0, LANES_V7X - GLA_RANK))).astype(BF16)
    wup = jnp.pad(w_up, ((0, LANES_V7X - GLA_RANK), (0, 0))).astype(BF16)
    args = (x, pre_g.reshape(1, D), win, wup, b_alpha.reshape(1, kw),
            g_head.reshape(1, GLA_DV), w_out.astype(BF16), post_g.reshape(1, D))
    tile_spec = pl.BlockSpec((B, TILE_T, D), lambda t: (0, t, 0))
    in_specs = [tile_spec] + [_const_spec(a.shape) for a in args[1:]]
    return pl.pallas_call(
        _gla_kernel,
        grid=(T // TILE_T,),
        in_specs=in_specs,
        out_specs=tile_spec,
        out_shape=jax.ShapeDtypeStruct((B, T, D), x.dtype),
        scratch_shapes=[pltpu.VMEM((B * GLA_HEADS, GLA_DV, GLA_DK), F32),
                        pltpu.VMEM((B, TILE_T, vw), F32)],
        compiler_params=pltpu.CompilerParams(
            dimension_semantics=("arbitrary",),
            vmem_limit_bytes=VMEM_LIMIT_BYTES),
        name="gla_layer",
    )(*args)


def _block_rows(x, bd_mask):
    return jnp.where(bd_mask, jnp.concatenate([x] * GROUP_HEADS, axis=0), jnp.zeros((), x.dtype))


def _rwkv_kernel(x_ref, pre_g_ref, win_ref, mu_ref, wup_ref, w0_ref, a0_ref,
                 kk_ref, ka_ref, rk_ref, lnw_ref, lnb_ref, gdown_ref, gup_ref, wout_ref, post_g_ref,
                 o_ref,
                 h_ref, prev_ref,
                 at_ref, rt_ref, bt_ref, kt_ref, bh_ref, kh_ref, vb_ref, gam_ref, y_ref):
    nb, tile, _ = x_ref.shape
    width = at_ref.shape[2]
    nc = tile // CHUNK
    ng = width // GROUP_LANES

    @pl.when(pl.program_id(0) == 0)
    def _():
        h_ref[...] = jnp.zeros_like(h_ref)
        prev_ref[...] = jnp.zeros_like(prev_ref)

    tri = _chunk_tri(tile)
    br = lax.broadcasted_iota(jnp.int32, (GROUP_LANES, GROUP_LANES), 0)
    bc = lax.broadcasted_iota(jnp.int32, (GROUP_LANES, GROUP_LANES), 1)
    bd_mask = (br // RWKV_HEAD) == (bc // RWKV_HEAD)
    diag = br == bc
    tr =lax.broadcasted_iota(jnp.int32, (CHUNK, GROUP_LANES), 0)
    ts = lax.broadcasted_iota(jnp.int32, (CHUNK, GROUP_LANES), 1) % CHUNK
    strict = ts < tr
    causal = ts <= tr
    eye = (ts == tr).astype(F32)
    cgs = [(c, g) for c in range(nc) for g in range(ng)]
    lanes_of = lambda g: slice(g * GROUP_LANES, (g + 1) * GROUP_LANES)
    blk = lambda t: _block_rows(t.astype(BF16), bd_mask)

    def head_sum(t, up_terms=1):
        return _dot_01_rhs(_dot_01_rhs(t, gdown_ref[...], 1), gup_ref[...], up_terms)

    cols_of = dict(r=slice(0, width), k=slice(width, 2 * width), v=slice(2 * width, 3 * width),
                   gate=slice(3 * width, 4 * width), lo=slice(4 * width, win_ref.shape[1]))

    def project(b, p):
        h = _rmsnorm(x_ref[b], pre_g_ref[...]).astype(BF16)
        for name in ("lo", "k", "r", "v", "gate"):
            p[name] = _dot(h, win_ref[:, cols_of[name]])
            yield

    def shifted_lerp(b, p, name):
        x = p[name]
        prev = prev_ref.at[b, :, cols_of[name]]
        first = lax.broadcasted_iota(jnp.int32, x.shape, 0) == 0
        shifted = jnp.where(first, prev[...], pltpu.roll(x, 1, 0))
        prev[...] = x[tile - 1:tile, :]
        return x + mu_ref[:, cols_of[name]] * (shifted - x)

    lerp_main = shifted_lerp

    def elementwise(b, p, e):
        p_lo = shifted_lerp(b, p, "lo")
        lo_lane = lax.broadcasted_iota(jnp.int32, p_lo.shape, 1)
        lo_act = jnp.where(lo_lane < RWKV_RANK, jnp.tanh(p_lo), p_lo).astype(BF16)
        up = _dot(lo_act, wup_ref[...])
        yield
        logw = -math.exp(-0.5) * _sigmoid(w0_ref[...] + up[:, :width])
        a = _sigmoid(a0_ref[...] + up[:, width:])
        k = lerp_main(b, p, "k")
        kk = k * kk_ref[...]
        kk_sq = head_sum(kk * kk)
        yield
        kk = kk / jnp.maximum(jnp.sqrt(kk_sq), 1e-12)
        k = k * (1.0 + (a - 1.0) * ka_ref[...])
        lcum = _dot_01_lhs(tri, logw)
        yield
        r = lerp_main(b, p, "r")
        v = lerp_main(b, p, "v")
        e["bonus"] = head_sum(r * k * rk_ref[...]) * v
        yield
        ltot = _chunk_total(lcum, tile)
        kka = kk * a
        e_neg = jnp.exp(-lcum)
        e_end = jnp.exp(ltot - lcum)
        at_ref[b] = (-kk * jnp.exp(lcum - logw)).astype(BF16)
        rt_ref[b] = (r * jnp.exp(lcum)).astype(BF16)
        bt_ref[b] = (kka * e_neg).astype(BF16)
        kt_ref[b] = (k * e_neg).astype(BF16)
        bh_ref[b] = (kka * e_end).astype(BF16)
        kh_ref[b] = (k * e_end).astype(BF16)
        vb_ref[b] = v.astype(BF16)
        gam_ref[b] = jnp.exp(ltot)
        e["gate"] = lerp_main(b, p, "gate")

    def chunk_algebra(b, alg):
        for key in ("ar_t", "vv", "m_rb", "t_inv", "akv", "y_k"):
            alg[key] = []
        for lo in range(0, len(cgs), CHAINS_PER_STAGE):
            part = cgs[lo:lo + CHAINS_PER_STAGE]
            ar_t = [jnp.concatenate([at_ref[b, _rows_of(c), lanes_of(g)],
                                     rt_ref[b, _rows_of(c), lanes_of(g)]], axis=0) for c, g in part]
            vv = [vb_ref[b, _rows_of(c), lanes_of(g)] for c, g in part]
            gram = [_dot_nt(ar_t[i], jnp.concatenate([blk(bt_ref[b, _rows_of(c), lanes_of(g)]),
                                                      blk(kt_ref[b, _rows_of(c), lanes_of(g)])], axis=0))
                    for i, (c, g) in enumerate(part)]
            yield
            l_ab = [jnp.where(strict, m[:CHUNK, :GROUP_LANES], 0.0) for m in gram]
            l_ak = [jnp.where(strict, m[:CHUNK, GROUP_LANES:], 0.0).astype(BF16) for m in gram]
            m_rb = [jnp.where(causal, m[CHUNK:, :GROUP_LANES], 0.0).astype(BF16) for m in gram]
            m_rk = [jnp.where(causal, m[CHUNK:, GROUP_LANES:], 0.0).astype(BF16) for m in gram]

            t_inv = [eye + m for m in l_ab]
            pw = [_dot(m.astype(BF16), blk(m)) for m in l_ab]
            yield
            for _ in range(int(math.log2(CHUNK)) - 2):
                both = [_dot(jnp.concatenate([t, p], axis=0).astype(BF16), blk(p)) for t, p in zip(t_inv, pw)]
                yield
                t_inv = [t + m[:CHUNK] for t, m in zip(t_inv, both)]
                pw = [m[CHUNK:] for m in both]
            t_inv = [t + _dot(t.astype(BF16), blk(p)) for t, p in zip(t_inv, pw)]
            yield

            akv_yk = [_dot(jnp.concatenate([l_ak[i], m_rk[i]], axis=0), blk(vv[i])) for i in range(len(part))]
            yield
            alg["ar_t"] += ar_t
            alg["vv"] += vv
            alg["m_rb"] += m_rb
            alg["t_inv"] += [t.astype(BF16) for t in t_inv]
            alg["akv"] += [m[:CHUNK] for m in akv_yk]
            alg["y_k"] += [m[CHUNK:] for m in akv_yk]

    def state_scan(alg):
        chains = [(b, g) for b in range(nb) for g in range(ng)]
        for c in range(nc):
            h = [h_ref[b * ng + g] for b, g in chains]
            wh = [_dot(alg[b]["ar_t"][c * ng + g], h[j].astype(BF16)) for j, (b, g) in enumerate(chains)]
            z = [wh[j][:CHUNK] + alg[b]["akv"][c * ng + g] for j, (b, g) in enumerate(chains)]
            u = [_dot(alg[b]["t_inv"][c * ng + g], blk(z[j])).astype(BF16) for j, (b, g) in enumerate(chains)]
            upd = [_dot_tn(jnp.concatenate([bh_ref[b, _rows_of(c), lanes_of(g)],
                                            kh_ref[b, _rows_of(c), lanes_of(g)]], axis=0),
                           jnp.concatenate([u[j], alg[b]["vv"][c * ng + g]], axis=0))
                   for j, (b, g) in enumerate(chains)]
            for j, (b, g) in enumerate(chains):
                i = c * ng + g
                gam_row = jnp.broadcast_to(gam_ref[b, c * CHUNK:c * CHUNK + 1, lanes_of(g)],
                                           (GROUP_LANES, GROUP_LANES))
                gam_col = jnp.sum(jnp.where(diag, gam_row, 0.0), axis=1, keepdims=True)
                h_ref[b * ng + g] = h[j] * gam_col + jnp.where(bd_mask, upd[j], 0.0)
                y_ref[b, _rows_of(c), lanes_of(g)] = (wh[j][CHUNK:] + alg[b]["y_k"][i]
                                                      + _dot(alg[b]["m_rb"][i], blk(u[j])))

    def output(b, e):
        y = y_ref[b]
        mean = head_sum(y, up_terms=2) * (1.0 / RWKV_HEAD)
        yield
        yc = y - mean
        var = head_sum(yc * yc) * (1.0 / RWKV_HEAD)
        yield
        y = yc * lax.rsqrt(var + GN_EPS) * lnw_ref[...] + lnb_ref[...]
        y = (y + e["bonus"]) * (e["gate"] * _sigmoid(e["gate"]))
        out = _dot(y.astype(BF16), wout_ref[...])
        yield
        o_ref[b] = x_ref[b] + _rmsnorm(out, post_g_ref[...])

    proj = [dict() for _ in range(nb)]
    elem = [dict() for _ in range(nb)]
    alg = [dict() for _ in range(nb)]
    _software_pipeline(nb, [lambda b: project(b, proj[b]),
                            lambda b: elementwise(b, proj[b], elem[b]),
                            lambda b: chunk_algebra(b, alg[b])])
    state_scan(alg)
    _interleave([output(b, elem[b]) for b in range(nb)])


def _rwkv_layer(x, pre_g, w_in, mu, w0, w_decay_up, a0, w_iclr_up, k_k, k_a, r_k, ln_w, ln_b, w_out, post_g):
    B, T, D = x.shape
    width = w_out.shape[0]
    heads = width // RWKV_HEAD
    rank2 = 2 * RWKV_RANK
    zeros = jnp.zeros((RWKV_RANK, width), F32)
    wup = jnp.concatenate([jnp.concatenate([w_decay_up, zeros], axis=1),
                           jnp.concatenate([zeros, w_iclr_up], axis=1)], axis=0).astype(BF16)
    head_of_lane = jnp.arange(width) // RWKV_HEAD
    gdown = (head_of_lane[:, None] == jnp.arange(LANES_V7X)[None, :]).astype(BF16)
    gup = gdown.T
    row = lambda t: t.reshape(1, -1)
    args = (x, row(pre_g), w_in.astype(BF16), row(mu), wup, row(w0), row(a0),
            row(k_k), row(k_a), row(r_k), row(ln_w), row(ln_b), gdown, gup, w_out.astype(BF16), row(post_g))
    tile_spec = pl.BlockSpec((B, TILE_T, D), lambda t: (0, t, 0))
    in_specs = [tile_spec] + [_const_spec(a.shape) for a in args[1:]]
    ngroups = width // GROUP_LANES
    scratch = [pltpu.VMEM((B * ngroups, GROUP_LANES, GROUP_LANES), F32),
               pltpu.VMEM((B, 1, w_in.shape[1]), F32)]
    scratch += [pltpu.VMEM((B, TILE_T, width), BF16)] * 7
    scratch += [pltpu.VMEM((B, TILE_T, width), F32)] * 2
    assert heads * RWKV_HEAD == width and rank2 == LANES_V7X
    return pl.pallas_call(
        _rwkv_kernel,
        grid=(T // TILE_T,),
        in_specs=in_specs,
        out_specs=tile_spec,
        out_shape=jax.ShapeDtypeStruct((B, T, D), x.dtype),
        scratch_shapes=scratch,
        compiler_params=pltpu.CompilerParams(
            dimension_semantics=("arbitrary",),
            vmem_limit_bytes=VMEM_LIMIT_BYTES),
        name="rwkv_layer",
    )(*args)


def kernel(x, gla_pre_norm, gla_w_in, gla_w_alpha_up, gla_b_alpha, gla_head_norm, gla_w_out, gla_post_norm,
           rwkv_pre_norm, rwkv_w_in, rwkv_mu, rwkv_w0, rwkv_w_decay_up, rwkv_a0, rwkv_w_iclr_up, rwkv_k_k,
           rwkv_k_a, rwkv_r_k, rwkv_ln_w, rwkv_ln_b, rwkv_w_out, rwkv_post_norm):
    depth = gla_pre_norm.shape[0] + rwkv_pre_norm.shape[0]
    for i in range(depth):
        j = i // 2
        if i % 2 == 0:
            x = _gla_layer(x, gla_pre_norm[j], gla_w_in[j], gla_w_alpha_up[j], gla_b_alpha[j],
                           gla_head_norm[j], gla_w_out[j], gla_post_norm[j])
        else:
            x = _rwkv_layer(x, rwkv_pre_norm[j], rwkv_w_in[j], rwkv_mu[j], rwkv_w0[j], rwkv_w_decay_up[j],
                            rwkv_a0[j], rwkv_w_iclr_up[j], rwkv_k_k[j], rwkv_k_a[j], rwkv_r_k[j].reshape(-1),
                            rwkv_ln_w[j], rwkv_ln_b[j], rwkv_w_out[j], rwkv_post_norm[j])
    return x
```

```python
import math

import jax
import jax.numpy as jnp
from jax import lax
from jax.experimental import pallas as pl
from jax.experimental.pallas import tpu as pltpu

F32 = jnp.float32
BF16 = jnp.bfloat16

LANES_V7X = 128
MXU_DIM_V7X = 256
VMEM_LIMIT_BYTES = 56 * 1024 * 1024

RMS_EPS = 1e-6
GN_EPS = 64e-5

CHUNK = 64
TILE_T = 256

GLA_HEADS = 4
GLA_DK = 128
GLA_DV = 256
GLA_RANK = 16
GLA_NORMALIZER = 16.0

RWKV_HEAD = 64
GROUP_HEADS = MXU_DIM_V7X // RWKV_HEAD
GROUP_LANES = GROUP_HEADS * RWKV_HEAD
RWKV_RANK = 64
CHAINS_PER_STAGE = 16


def _dot(a, b):
    return jnp.dot(a, b, preferred_element_type=F32)


def _dot_nt(a, b):
    return lax.dot_general(a, b, (((1,), (1,)), ((), ())), preferred_element_type=F32)


def _dot_tn(a, b):
    return lax.dot_general(a, b, (((0,), (0,)), ((), ())), preferred_element_type=F32)


def _split2(x):
    x1 = x.astype(BF16)
    return x1, (x - x1.astype(F32)).astype(BF16)


def _dot_01_lhs(m01, x):
    x1, x2 = _split2(x)
    return _dot(m01, x1) + _dot(m01, x2)


def _dot_01_rhs(x, m01, terms):
    if terms == 1:
        return _dot(x.astype(BF16), m01)
    x1, x2 = _split2(x)
    return _dot(x1, m01) + _dot(x2, m01)


def _rmsnorm(x, g):
    ms = jnp.mean(x * x, axis=-1, keepdims=True)
    return x * lax.rsqrt(ms + RMS_EPS) * g


def _sigmoid(z):
    return 1.0 / (1.0 + jnp.exp(-z))


def _chunk_tri(tile):
    r = lax.broadcasted_iota(jnp.int32, (tile, tile), 0)
    c = lax.broadcasted_iota(jnp.int32, (tile, tile), 1)
    return ((r // CHUNK == c // CHUNK) & (c <= r)).astype(BF16)


def _chunk_total(cum, tile):
    nc = tile // CHUNK
    width = cum.shape[-1]
    c3 = cum.reshape(nc, CHUNK, width)
    last = c3[:, CHUNK - 1:CHUNK, :]
    return jnp.broadcast_to(last, (nc, CHUNK, width)).reshape(tile, width)


def _rows_of(c):
    return slice(c * CHUNK, (c + 1) * CHUNK)


def _interleave(gens):
    gens = list(gens)
    while gens:
        alive = []
        for g in gens:
            try:
                next(g)
                alive.append(g)
            except StopIteration:
                pass
        gens = alive


def _software_pipeline(nb, phases):
    for step in range(nb + len(phases) - 1):
        _interleave(phases[k](step - k) for k in reversed(range(len(phases))) if 0 <= step - k < nb)


def _gla_kernel(x_ref, pre_g_ref, win_ref, wup_ref, b_ref,
                ghead_ref, wout_ref, post_g_ref, o_ref, st_ref, o_acc_ref):
    nb, tile, _ = x_ref.shape
    nc = tile // CHUNK

    @pl.when(pl.program_id(0) == 0)
    def _():
        st_ref[...] = jnp.zeros_like(st_ref)

    tri = _chunk_tri(tile)
    row = lax.broadcasted_iota(jnp.int32, (CHUNK, CHUNK), 0)
    col = lax.broadcasted_iota(jnp.int32, (CHUNK, CHUNK), 1)
    causal = col <= row
    pairs =[(c, hd) for c in range(nc) for hd in range(GLA_HEADS)]
    kl_of = lambda hd: slice(hd * GLA_DK, (hd + 1) * GLA_DK)
    vl_of = lambda hd: slice(hd * GLA_DV, (hd + 1) * GLA_DV)

    def project(b, p):
        h = _rmsnorm(x_ref[b], pre_g_ref[...]).astype(BF16)
        kw, vw = GLA_HEADS * GLA_DK, GLA_HEADS * GLA_DV
        cols = dict(q=(0, kw), k=(kw, 2 * kw), v=(2 * kw, 2 * kw + vw), gate=(2 * kw + vw, 2 * kw + 2 * vw),
                    ad=(2 * kw + 2 * vw, win_ref.shape[1]))
        for name in ("ad", "k", "q", "v", "gate"):
            p[name] = _dot(h, win_ref[:, cols[name][0]:cols[name][1]])
            yield

    def decays(b, p, d):
        z = _dot(p["ad"].astype(BF16), wup_ref[...]) + b_ref[...]
        yield
        log_alpha = (jnp.minimum(z, 0.0) - jnp.log1p(jnp.exp(-jnp.abs(z)))) * (1.0 / GLA_NORMALIZER)
        bcum = _dot_01_lhs(tri, log_alpha)
        yield
        btot = _chunk_total(bcum, tile)
        d.update(q_dec=(p["q"] * (GLA_DK ** -0.5) * jnp.exp(bcum)).astype(BF16),
                 k_dec=(p["k"] * jnp.exp(-bcum)).astype(BF16),
                 k_end=(p["k"] * jnp.exp(btot - bcum)).astype(BF16),
                 chunk_decay=jnp.exp(btot), vb=p["v"].astype(BF16))

    def chunks(b, d):
        qd = [d["q_dec"][_rows_of(c), kl_of(hd)] for c, hd in pairs]
        vc = [d["vb"][_rows_of(c), vl_of(hd)] for c, hd in pairs]
        att = [jnp.where(causal, _dot_nt(qd[i], d["k_dec"][_rows_of(c), kl_of(hd)]), 0.0).astype(BF16)
               for i, (c, hd) in enumerate(pairs)]
        yield
        kv = [_dot_tn(vc[i], d["k_end"][_rows_of(c), kl_of(hd)]) for i, (c, hd) in enumerate(pairs)]
        yield
        o_intra = [_dot(att[i], vc[i]) for i in range(len(pairs))]
        yield
        st_in = [None] * len(pairs)
        for hd in range(GLA_HEADS):
            st = st_ref[b * GLA_HEADS + hd]
            for c in range(nc):
                i = c * GLA_HEADS + hd
                st_in[i] = st.astype(BF16)
                st = st * d["chunk_decay"][c * CHUNK:c * CHUNK + 1, kl_of(hd)] + kv[i]
            st_ref[b * GLA_HEADS + hd] = st
        for i, (c, hd) in enumerate(pairs):
            o_acc_ref[b, _rows_of(c), vl_of(hd)] = o_intra[i] + _dot_nt(qd[i], st_in[i])

    def output(b, p):
        o = o_acc_ref[b]
        outs = []
        for hd in range(GLA_HEADS):
            oh = o[:, vl_of(hd)]
            outs.append(oh * lax.rsqrt(jnp.mean(oh * oh, axis=-1, keepdims=True) + RMS_EPS) * ghead_ref[...])
        o = jnp.concatenate(outs, axis=-1) * (p["gate"] * _sigmoid(p["gate"]))
        y = _dot(o.astype(BF16), wout_ref[...])
        yield
        o_ref[b] = x_ref[b] + _rmsnorm(y, post_g_ref[...])

    proj = [dict() for _ in range(nb)]
    dec = [dict() for _ in range(nb)]
    _software_pipeline(nb, [lambda b: project(b, proj[b]),
                            lambda b: decays(b, proj[b], dec[b]),
                            lambda b: chunks(b, dec[b])])
    _interleave([output(b, proj[b]) for b in range(nb)])


def _const_spec(shape):
    nd = len(shape)
    return pl.BlockSpec(shape, lambda t: (0,) * nd, pipeline_mode=pl.Buffered(1))


def _gla_layer(x, pre_g, w_in, w_up, b_alpha, g_head, w_out, post_g):
    B, T, D = x.shape
    kw = GLA_HEADS * GLA_DK
    vw = GLA_HEADS * GLA_DV
    win = jnp.pad(w_in, ((0, 0), (0, LANES_V7X - GLA_RANK))).astype(BF16)
    wup = jnp.pad(w_up, ((0, LANES_V7X - GLA_RANK), (0, 0))).astype(BF16)
    args = (x, pre_g.reshape(1, D), win, wup, b_alpha.reshape(1, kw),
            g_head.reshape(1, GLA_DV), w_out.astype(BF16), post_g.reshape(1, D))
    tile_spec = pl.BlockSpec((B, TILE_T, D), lambda t: (0, t, 0))
    in_specs = [tile_spec] + [_const_spec(a.shape) for a in args[1:]]
    return pl.pallas_call(
        _gla_kernel,
        grid=(T // TILE_T,),
        in_specs=in_specs,
        out_specs=tile_spec,
        out_shape=jax.ShapeDtypeStruct((B, T, D), x.dtype),
        scratch_shapes=[pltpu.VMEM((B * GLA_HEADS, GLA_DV, GLA_DK), F32),
                        pltpu.VMEM((B, TILE_T, vw), F32)],
        compiler_params=pltpu.CompilerParams(
            dimension_semantics=("arbitrary",),
            vmem_limit_bytes=VMEM_LIMIT_BYTES),
        name="gla_layer",
    )(*args)


def _block_rows(x, bd_mask):
    return jnp.where(bd_mask, jnp.concatenate([x] * GROUP_HEADS, axis=0), jnp.zeros((), x.dtype))


def _rwkv_kernel(x_ref, pre_g_ref, win_ref, mu_ref, wup_ref, w0_ref, a0_ref,
                 kk_ref, ka_ref, rk_ref, lnw_ref, lnb_ref, gdown_ref, gup_ref, wout_ref, post_g_ref,
                 o_ref,
                 h_ref, prev_ref,
                 at_ref, rt_ref, bt_ref, kt_ref, bh_ref, kh_ref, vb_ref, gam_ref, y_ref):
    nb, tile, _ = x_ref.shape
    width = at_ref.shape[2]
    nc = tile // CHUNK
    ng = width // GROUP_LANES

    @pl.when(pl.program_id(0) == 0)
    def _():
        h_ref[...] = jnp.zeros_like(h_ref)
        prev_ref[...] = jnp.zeros_like(prev_ref)

    tri = _chunk_tri(tile)
    br = lax.broadcasted_iota(jnp.int32, (GROUP_LANES, GROUP_LANES), 0)
    bc = lax.broadcasted_iota(jnp.int32, (GROUP_LANES, GROUP_LANES), 1)
    bd_mask = (br // RWKV_HEAD) == (bc // RWKV_HEAD)
    diag = br == bc
    tr =lax.broadcasted_iota(jnp.int32, (CHUNK, GROUP_LANES), 0)
    ts = lax.broadcasted_iota(jnp.int32, (CHUNK, GROUP_LANES), 1) % CHUNK
    strict = ts < tr
    causal = ts <= tr
    eye = (ts == tr).astype(F32)
    cgs = [(c, g) for c in range(nc) for g in range(ng)]
    lanes_of = lambda g: slice(g * GROUP_LANES, (g + 1) * GROUP_LANES)
    blk = lambda t: _block_rows(t.astype(BF16), bd_mask)

    def head_sum(t, up_terms=1):
        return _dot_01_rhs(_dot_01_rhs(t, gdown_ref[...], 1), gup_ref[...], up_terms)

    cols_of = dict(r=slice(0, width), k=slice(width, 2 * width), v=slice(2 * width, 3 * width),
                   gate=slice(3 * width, 4 * width), lo=slice(4 * width, win_ref.shape[1]))

    def project(b, p):
        h = _rmsnorm(x_ref[b], pre_g_ref[...]).astype(BF16)
        for name in ("lo", "k", "r", "v", "gate"):
            p[name] = _dot(h, win_ref[:, cols_of[name]])
            yield

    def shifted_lerp(b, p, name):
        x = p[name]
        prev = prev_ref.at[b, :, cols_of[name]]
        first = lax.broadcasted_iota(jnp.int32, x.shape, 0) == 0
        shifted = jnp.where(first, prev[...], pltpu.roll(x, 1, 0))
        prev[...] = x[tile - 1:tile, :]
        return x + mu_ref[:, cols_of[name]] * (shifted - x)

    lerp_main = shifted_lerp

    def elementwise(b, p, e):
        p_lo = shifted_lerp(b, p, "lo")
        lo_lane = lax.broadcasted_iota(jnp.int32, p_lo.shape, 1)
        lo_act = jnp.where(lo_lane < RWKV_RANK, jnp.tanh(p_lo), p_lo).astype(BF16)
        up = _dot(lo_act, wup_ref[...])
        yield
        logw = -math.exp(-0.5) * _sigmoid(w0_ref[...] + up[:, :width])
        a = _sigmoid(a0_ref[...] + up[:, width:])
        k = lerp_main(b, p, "k")
        kk = k * kk_ref[...]
        kk_sq = head_sum(kk * kk)
        yield
        kk = kk / jnp.maximum(jnp.sqrt(kk_sq), 1e-12)
        k = k * (1.0 + (a - 1.0) * ka_ref[...])
        lcum = _dot_01_lhs(tri, logw)
        yield
        r = lerp_main(b, p, "r")
        v = lerp_main(b, p, "v")
        e["bonus"] = head_sum(r * k * rk_ref[...]) * v
        yield
        ltot = _chunk_total(lcum, tile)
        kka = kk * a
        e_neg = jnp.exp(-lcum)
        e_end = jnp.exp(ltot - lcum)
        at_ref[b] = (-kk * jnp.exp(lcum - logw)).astype(BF16)
        rt_ref[b] = (r * jnp.exp(lcum)).astype(BF16)
        bt_ref[b] = (kka * e_neg).astype(BF16)
        kt_ref[b] = (k * e_neg).astype(BF16)
        bh_ref[b] = (kka * e_end).astype(BF16)
        kh_ref[b] = (k * e_end).astype(BF16)
        vb_ref[b] = v.astype(BF16)
        gam_ref[b] = jnp.exp(ltot)
        e["gate"] = lerp_main(b, p, "gate")

    def chunk_algebra(b, alg):
        for key in ("ar_t", "vv", "t_mt", "gam_col", "akv", "y_k"):
            alg[key] = []
        for lo in range(0, len(cgs), CHAINS_PER_STAGE):
            part = cgs[lo:lo + CHAINS_PER_STAGE]
            ar_t = [jnp.concatenate([at_ref[b, _rows_of(c), lanes_of(g)],
                                     rt_ref[b, _rows_of(c), lanes_of(g)]], axis=0) for c, g in part]
            vv = [vb_ref[b, _rows_of(c), lanes_of(g)] for c, g in part]
            gram = [_dot_nt(ar_t[i], jnp.concatenate([blk(bt_ref[b, _rows_of(c), lanes_of(g)]),
                                                      blk(kt_ref[b, _rows_of(c), lanes_of(g)])], axis=0))
                    for i, (c, g) in enumerate(part)]
            yield
            l_ab = [jnp.where(strict, m[:CHUNK, :GROUP_LANES], 0.0) for m in gram]
            l_ak = [jnp.where(strict, m[:CHUNK, GROUP_LANES:], 0.0).astype(BF16) for m in gram]
            m_rb = [jnp.where(causal, m[CHUNK:, :GROUP_LANES], 0.0).astype(BF16) for m in gram]
            m_rk = [jnp.where(causal, m[CHUNK:, GROUP_LANES:], 0.0).astype(BF16) for m in gram]

            t_inv = [eye + m for m in l_ab]
            pw = [_dot(m.astype(BF16), blk(m)) for m in l_ab]
            yield
            for _ in range(int(math.log2(CHUNK)) - 2):
                both = [_dot(jnp.concatenate([t, p], axis=0).astype(BF16), blk(p)) for t, p in zip(t_inv, pw)]
                yield
                t_inv = [t + m[:CHUNK] for t, m in zip(t_inv, both)]
                pw = [m[CHUNK:] for m in both]
            t_inv = [t + _dot(t.astype(BF16), blk(p)) for t, p in zip(t_inv, pw)]
            yield

            akv_yk = [_dot(jnp.concatenate([l_ak[i], m_rk[i]], axis=0), blk(vv[i])) for i in range(len(part))]
            yield
            t_mt = [jnp.concatenate([t_inv[i], _dot(m_rb[i], blk(t_inv[i]))], axis=0).astype(BF16)
                    for i in range(len(part))]
            yield
            gam_col = []
            for c, g in part:
                gam_row = jnp.broadcast_to(gam_ref[b, c * CHUNK:c * CHUNK + 1, lanes_of(g)],
                                           (GROUP_LANES, GROUP_LANES))
                gam_col.append(jnp.sum(jnp.where(diag, gam_row, 0.0), axis=1, keepdims=True))
            alg["ar_t"] += ar_t
            alg["vv"] += vv
            alg["t_mt"] += t_mt
            alg["gam_col"] += gam_col
            alg["akv"] += [m[:CHUNK] for m in akv_yk]
            alg["y_k"] += [m[CHUNK:] for m in akv_yk]

    def state_scan(alg):
        chains = [(b, g) for b in range(nb) for g in range(ng)]
        for c in range(nc):
            h = [h_ref[b * ng + g] for b, g in chains]
            wh = [_dot(alg[b]["ar_t"][c * ng + g], h[j].astype(BF16)) for j, (b, g) in enumerate(chains)]
            z = [wh[j][:CHUNK] + alg[b]["akv"][c * ng + g] for j, (b, g) in enumerate(chains)]
            uy = [_dot(alg[b]["t_mt"][c * ng + g], blk(z[j])) for j, (b, g) in enumerate(chains)]
            upd = [_dot_tn(jnp.concatenate([bh_ref[b, _rows_of(c), lanes_of(g)],
                                            kh_ref[b, _rows_of(c), lanes_of(g)]], axis=0),
                           jnp.concatenate([uy[j][:CHUNK].astype(BF16), alg[b]["vv"][c * ng + g]], axis=0))
                   for j, (b, g) in enumerate(chains)]
            for j, (b, g) in enumerate(chains):
                i = c * ng + g
                h_ref[b * ng + g] = h[j] * alg[b]["gam_col"][i] + jnp.where(bd_mask, upd[j], 0.0)
                y_ref[b, _rows_of(c), lanes_of(g)] = wh[j][CHUNK:] + alg[b]["y_k"][i] + uy[j][CHUNK:]

    def output(b, e):
        y = y_ref[b]
        mean = head_sum(y, up_terms=2) * (1.0 / RWKV_HEAD)
        yield
        yc = y - mean
        var = head_sum(yc * yc) * (1.0 / RWKV_HEAD)
        yield
        y = yc * lax.rsqrt(var + GN_EPS) * lnw_ref[...] + lnb_ref[...]
        y = (y + e["bonus"]) * (e["gate"] * _sigmoid(e["gate"]))
        out = _dot(y.astype(BF16), wout_ref[...])
        yield
        o_ref[b] = x_ref[b] + _rmsnorm(out, post_g_ref[...])

    proj = [dict() for _ in range(nb)]
    elem = [dict() for _ in range(nb)]
    alg = [dict() for _ in range(nb)]
    _software_pipeline(nb, [lambda b: project(b, proj[b]),
                            lambda b: elementwise(b, proj[b], elem[b]),
                            lambda b: chunk_algebra(b, alg[b])])
    state_scan(alg)
    _interleave([output(b, elem[b]) for b in range(nb)])


def _rwkv_layer(x, pre_g, w_in, mu, w0, w_decay_up, a0, w_iclr_up, k_k, k_a, r_k, ln_w, ln_b, w_out, post_g):
    B, T, D = x.shape
    width = w_out.shape[0]
    heads = width // RWKV_HEAD
    rank2 = 2 * RWKV_RANK
    zeros = jnp.zeros((RWKV_RANK, width), F32)
    wup = jnp.concatenate([jnp.concatenate([w_decay_up, zeros], axis=1),
                           jnp.concatenate([zeros, w_iclr_up], axis=1)], axis=0).astype(BF16)
    head_of_lane = jnp.arange(width) // RWKV_HEAD
    gdown = (head_of_lane[:, None] == jnp.arange(LANES_V7X)[None, :]).astype(BF16)
    gup = gdown.T
    row = lambda t: t.reshape(1, -1)
    args = (x, row(pre_g), w_in.astype(BF16), row(mu), wup, row(w0), row(a0),
            row(k_k), row(k_a), row(r_k), row(ln_w), row(ln_b), gdown, gup, w_out.astype(BF16), row(post_g))
    tile_spec = pl.BlockSpec((B, TILE_T, D), lambda t: (0, t, 0))
    in_specs = [tile_spec] + [_const_spec(a.shape) for a in args[1:]]
    ngroups = width // GROUP_LANES
    scratch = [pltpu.VMEM((B * ngroups, GROUP_LANES, GROUP_LANES), F32),
               pltpu.VMEM((B, 1, w_in.shape[1]), F32)]
    scratch += [pltpu.VMEM((B, TILE_T, width), BF16)] * 7
    scratch += [pltpu.VMEM((B, TILE_T, width), F32)] * 2
    assert heads * RWKV_HEAD == width and rank2 == LANES_V7X
    return pl.pallas_call(
        _rwkv_kernel,
        grid=(T // TILE_T,),
        in_specs=in_specs,
        out_specs=tile_spec,
        out_shape=jax.ShapeDtypeStruct((B, T, D), x.dtype),
        scratch_shapes=scratch,
        compiler_params=pltpu.CompilerParams(
            dimension_semantics=("arbitrary",),
            vmem_limit_bytes=VMEM_LIMIT_BYTES),
        name="rwkv_layer",
    )(*args)


def kernel(x, gla_pre_norm, gla_w_in, gla_w_alpha_up, gla_b_alpha, gla_head_norm, gla_w_out, gla_post_norm,
           rwkv_pre_norm, rwkv_w_in, rwkv_mu, rwkv_w0, rwkv_w_decay_up, rwkv_a0, rwkv_w_iclr_up, rwkv_k_k,
           rwkv_k_a, rwkv_r_k, rwkv_ln_w, rwkv_ln_b, rwkv_w_out, rwkv_post_norm):
    depth = gla_pre_norm.shape[0] + rwkv_pre_norm.shape[0]
    for i in range(depth):
        j = i // 2
        if i % 2 == 0:
            x = _gla_layer(x, gla_pre_norm[j], gla_w_in[j], gla_w_alpha_up[j], gla_b_alpha[j],
                           gla_head_norm[j], gla_w_out[j], gla_post_norm[j])
        else:
            x = _rwkv_layer(x, rwkv_pre_norm[j], rwkv_w_in[j], rwkv_mu[j], rwkv_w0[j], rwkv_w_decay_up[j],
                            rwkv_a0[j], rwkv_w_iclr_up[j], rwkv_k_k[j], rwkv_k_a[j], rwkv_r_k[j].reshape(-1),
                            rwkv_ln_w[j], rwkv_ln_b[j], rwkv_w_out[j], rwkv_post_norm[j])
    return x
```

```python
import math

import jax
import jax.numpy as jnp
from jax import lax
from jax.experimental import pallas as pl
from jax.experimental.pallas import tpu as pltpu

F32 = jnp.float32
BF16 = jnp.bfloat16

LANES_V7X = 128
MXU_DIM_V7X = 256
VMEM_LIMIT_BYTES = 56 * 1024 * 1024

RMS_EPS = 1e-6
GN_EPS = 64e-5

CHUNK = 64
TILE_T = 256

GLA_HEADS = 4
GLA_DK = 128
GLA_DV = 256
GLA_RANK = 16
GLA_NORMALIZER = 16.0

RWKV_HEAD = 64
GROUP_HEADS = MXU_DIM_V7X // RWKV_HEAD
GROUP_LANES = GROUP_HEADS * RWKV_HEAD
RWKV_RANK = 64
CHAINS_PER_STAGE = 16


def _dot(a, b):
    return jnp.dot(a, b, preferred_element_type=F32)


def _dot_nt(a, b):
    return lax.dot_general(a, b, (((1,), (1,)), ((), ())), preferred_element_type=F32)


def _dot_tn(a, b):
    return lax.dot_general(a, b, (((0,), (0,)), ((), ())), preferred_element_type=F32)


def _split2(x):
    x1 = x.astype(BF16)
    return x1, (x - x1.astype(F32)).astype(BF16)


def _dot_01_lhs(m01, x):
    x1, x2 = _split2(x)
    return _dot(m01, x1) + _dot(m01, x2)


def _rmsnorm(x, g):
    ms = jnp.mean(x * x, axis=-1, keepdims=True)
    return x * lax.rsqrt(ms + RMS_EPS) * g


def _sigmoid(z):
    return 1.0 / (1.0 + jnp.exp(-z))


def _chunk_tri(tile):
    r = lax.broadcasted_iota(jnp.int32, (tile, tile), 0)
    c = lax.broadcasted_iota(jnp.int32, (tile, tile), 1)
    return ((r // CHUNK == c // CHUNK) & (c <= r)).astype(BF16)


def _chunk_total(cum, tile):
    nc = tile // CHUNK
    width = cum.shape[-1]
    c3 = cum.reshape(nc, CHUNK, width)
    last = c3[:, CHUNK - 1:CHUNK, :]
    return jnp.broadcast_to(last, (nc, CHUNK, width)).reshape(tile, width)


def _rows_of(c):
    return slice(c * CHUNK, (c + 1) * CHUNK)


def _interleave(gens):
    gens = list(gens)
    while gens:
        alive = []
        for g in gens:
            try:
                next(g)
                alive.append(g)
            except StopIteration:
                pass
        gens = alive


def _software_pipeline(nb, phases):
    for step in range(nb + len(phases) - 1):
        _interleave(phases[k](step - k) for k in reversed(range(len(phases))) if 0 <= step - k < nb)


def _gla_kernel(x_ref, pre_g_ref, win_ref, wup_ref, b_ref,
                ghead_ref, wout_ref, post_g_ref, o_ref, st_ref, o_acc_ref):
    nb, tile, _ = x_ref.shape
    nc = tile // CHUNK

    @pl.when(pl.program_id(0) == 0)
    def _():
        st_ref[...] = jnp.zeros_like(st_ref)

    tri = _chunk_tri(tile)
    row = lax.broadcasted_iota(jnp.int32, (CHUNK, CHUNK), 0)
    col = lax.broadcasted_iota(jnp.int32, (CHUNK, CHUNK), 1)
    causal = col <= row
    pairs =[(c, hd) for c in range(nc) for hd in range(GLA_HEADS)]
    kl_of = lambda hd: slice(hd * GLA_DK, (hd + 1) * GLA_DK)
    vl_of = lambda hd: slice(hd * GLA_DV, (hd + 1) * GLA_DV)

    def project(b, p):
        h = _rmsnorm(x_ref[b], pre_g_ref[...]).astype(BF16)
        kw, vw = GLA_HEADS * GLA_DK, GLA_HEADS * GLA_DV
        cols = dict(q=(0, kw), k=(kw, 2 * kw), v=(2 * kw, 2 * kw + vw), gate=(2 * kw + vw, 2 * kw + 2 * vw),
                    ad=(2 * kw + 2 * vw, win_ref.shape[1]))
        for name in ("ad", "k", "q", "v", "gate"):
            p[name] = _dot(h, win_ref[:, cols[name][0]:cols[name][1]])
            yield

    def decays(b, p, d):
        z = _dot(p["ad"].astype(BF16), wup_ref[...]) + b_ref[...]
        yield
        log_alpha = (jnp.minimum(z, 0.0) - jnp.log1p(jnp.exp(-jnp.abs(z)))) * (1.0 / GLA_NORMALIZER)
        bcum = _dot_01_lhs(tri, log_alpha)
        yield
        btot = _chunk_total(bcum, tile)
        d.update(q_dec=(p["q"] * (GLA_DK ** -0.5) * jnp.exp(bcum)).astype(BF16),
                 k_dec=(p["k"] * jnp.exp(-bcum)).astype(BF16),
                 k_end=(p["k"] * jnp.exp(btot - bcum)).astype(BF16),
                 chunk_decay=jnp.exp(btot), vb=p["v"].astype(BF16))

    def chunks(b, d):
        qd = [d["q_dec"][_rows_of(c), kl_of(hd)] for c, hd in pairs]
        vc = [d["vb"][_rows_of(c), vl_of(hd)] for c, hd in pairs]
        att = [jnp.where(causal, _dot_nt(qd[i], d["k_dec"][_rows_of(c), kl_of(hd)]), 0.0).astype(BF16)
               for i, (c, hd) in enumerate(pairs)]
        yield
        kv = [_dot_tn(vc[i], d["k_end"][_rows_of(c), kl_of(hd)]) for i, (c, hd) in enumerate(pairs)]
        yield
        o_intra = [_dot(att[i], vc[i]) for i in range(len(pairs))]
        yield
        st_in = [None] * len(pairs)
        for hd in range(GLA_HEADS):
            st = st_ref[b * GLA_HEADS + hd]
            for c in range(nc):
                i = c * GLA_HEADS + hd
                st_in[i] = st.astype(BF16)
                st = st * d["chunk_decay"][c * CHUNK:c * CHUNK + 1, kl_of(hd)] + kv[i]
            st_ref[b * GLA_HEADS + hd] = st
        for i, (c, hd) in enumerate(pairs):
            o_acc_ref[b, _rows_of(c), vl_of(hd)] = o_intra[i] + _dot_nt(qd[i], st_in[i])

    def output(b, p):
        o = o_acc_ref[b]
        outs = []
        for hd in range(GLA_HEADS):
            oh = o[:, vl_of(hd)]
            outs.append(oh * lax.rsqrt(jnp.mean(oh * oh, axis=-1, keepdims=True) + RMS_EPS) * ghead_ref[...])
        o = jnp.concatenate(outs, axis=-1) * (p["gate"] * _sigmoid(p["gate"]))
        y = _dot(o.astype(BF16), wout_ref[...])
        yield
        o_ref[b] = x_ref[b] + _rmsnorm(y, post_g_ref[...])

    proj = [dict() for _ in range(nb)]
    dec = [dict() for _ in range(nb)]
    _software_pipeline(nb, [lambda b: project(b, proj[b]),
                            lambda b: decays(b, proj[b], dec[b]),
                            lambda b: chunks(b, dec[b])])
    _interleave([output(b, proj[b]) for b in range(nb)])


def _const_spec(shape):
    nd = len(shape)
    return pl.BlockSpec(shape, lambda t: (0,) * nd, pipeline_mode=pl.Buffered(1))


def _gla_layer(x, pre_g, w_in, w_up, b_alpha, g_head, w_out, post_g):
    B, T, D = x.shape
    kw = GLA_HEADS * GLA_DK
    vw = GLA_HEADS * GLA_DV
    win = jnp.pad(w_in, ((0, 0), (0, LANES_V7X - GLA_RANK))).astype(BF16)
    wup = jnp.pad(w_up, ((0, LANES_V7X - GLA_RANK), (0, 0))).astype(BF16)
    args = (x, pre_g.reshape(1, D), win, wup, b_alpha.reshape(1, kw),
            g_head.reshape(1, GLA_DV), w_out.astype(BF16), post_g.reshape(1, D))
    tile_spec = pl.BlockSpec((B, TILE_T, D), lambda t: (0, t, 0))
    in_specs = [tile_spec] + [_const_spec(a.shape) for a in args[1:]]
    return pl.pallas_call(
        _gla_kernel,
        grid=(T // TILE_T,),
        in_specs=in_specs,
        out_specs=tile_spec,
        out_shape=jax.ShapeDtypeStruct((B, T, D), x.dtype),
        scratch_shapes=[pltpu.VMEM((B * GLA_HEADS, GLA_DV, GLA_DK), F32),
                        pltpu.VMEM((B, TILE_T, vw), F32)],
        compiler_params=pltpu.CompilerParams(
            dimension_semantics=("arbitrary",),
            vmem_limit_bytes=VMEM_LIMIT_BYTES),
        name="gla_layer",
    )(*args)


def _block_rows(x, bd_mask):
    return jnp.where(bd_mask, jnp.concatenate([x] * GROUP_HEADS, axis=0), jnp.zeros((), x.dtype))


def _rwkv_kernel(x_ref, pre_g_ref, win_ref, mu_ref, wup_ref, w0_ref, a0_ref,
                 kk_ref, ka_ref, rk_ref, lnw_ref, lnb_ref, wout_ref, post_g_ref,
                 o_ref,
                 h_ref, prev_ref,
                 at_ref, rt_ref, bt_ref, kt_ref, bh_ref, kh_ref, vb_ref, gam_ref, y_ref):
    nb, tile, _ = x_ref.shape
    width = at_ref.shape[2]
    nc = tile // CHUNK
    ng = width // GROUP_LANES

    @pl.when(pl.program_id(0) == 0)
    def _():
        h_ref[...] = jnp.zeros_like(h_ref)
        prev_ref[...] = jnp.zeros_like(prev_ref)

    tri = _chunk_tri(tile)
    br = lax.broadcasted_iota(jnp.int32, (GROUP_LANES, GROUP_LANES), 0)
    bc = lax.broadcasted_iota(jnp.int32, (GROUP_LANES, GROUP_LANES), 1)
    bd_mask = (br // RWKV_HEAD) == (bc // RWKV_HEAD)
    diag = br == bc
    tr =lax.broadcasted_iota(jnp.int32, (CHUNK, GROUP_LANES), 0)
    ts = lax.broadcasted_iota(jnp.int32, (CHUNK, GROUP_LANES), 1) % CHUNK
    strict = ts < tr
    causal = ts <= tr
    eye = (ts == tr).astype(F32)
    cgs = [(c, g) for c in range(nc) for g in range(ng)]
    lanes_of = lambda g: slice(g * GROUP_LANES, (g + 1) * GROUP_LANES)
    blk = lambda t: _block_rows(t.astype(BF16), bd_mask)

    head_ones = bd_mask.astype(BF16)

    def head_sum(t):
        tb = t.astype(BF16)
        return jnp.concatenate([_dot(tb[:, lanes_of(g)], head_ones) for g in range(ng)], axis=1)

    cols_of = dict(r=slice(0, width), k=slice(width, 2 * width), v=slice(2 * width, 3 * width),
                   gate=slice(3 * width, 4 * width), lo=slice(4 * width, win_ref.shape[1]))

    def project(b, p):
        h = _rmsnorm(x_ref[b], pre_g_ref[...]).astype(BF16)
        for name in ("lo", "k", "r", "v", "gate"):
            p[name] = _dot(h, win_ref[:, cols_of[name]])
            yield

    def shifted_lerp(b, p, name):
        x = p[name]
        prev = prev_ref.at[b, :, cols_of[name]]
        first = lax.broadcasted_iota(jnp.int32, x.shape, 0) == 0
        shifted = jnp.where(first, prev[...], pltpu.roll(x, 1, 0))
        prev[...] = x[tile - 1:tile, :]
        return x + mu_ref[:, cols_of[name]] * (shifted - x)

    def elementwise(b, p, e):
        p_lo = shifted_lerp(b, p, "lo")
        lo_lane = lax.broadcasted_iota(jnp.int32, p_lo.shape, 1)
        lo_act = jnp.where(lo_lane < RWKV_RANK, jnp.tanh(p_lo), p_lo).astype(BF16)
        up = _dot(lo_act, wup_ref[...])
        yield
        logw = -math.exp(-0.5) * _sigmoid(w0_ref[...] + up[:, :width])
        a = _sigmoid(a0_ref[...] + up[:, width:])
        k = shifted_lerp(b, p,"k")
        kk = k * kk_ref[...]
        kk_sq = head_sum(kk * kk)
        yield
        kk = kk / jnp.maximum(jnp.sqrt(kk_sq), 1e-12)
        k = k * (1.0 + (a - 1.0) * ka_ref[...])
        lcum = _dot_01_lhs(tri, logw)
        yield
        r = shifted_lerp(b, p,"r")
        v = shifted_lerp(b, p,"v")
        e["bonus"] = head_sum(r * k * rk_ref[...]) * v
        yield
        ltot = _chunk_total(lcum, tile)
        kka = kk * a
        e_neg = jnp.exp(-lcum)
        e_end = jnp.exp(ltot - lcum)
        at_ref[b] = (-kk * jnp.exp(lcum - logw)).astype(BF16)
        rt_ref[b] = (r * jnp.exp(lcum)).astype(BF16)
        bt_ref[b] = (kka * e_neg).astype(BF16)
        kt_ref[b] = (k * e_neg).astype(BF16)
        bh_ref[b] = (kka * e_end).astype(BF16)
        kh_ref[b] = (k * e_end).astype(BF16)
        vb_ref[b] = v.astype(BF16)
        gam_ref[b] = jnp.exp(ltot)
        e["gate"] = shifted_lerp(b, p,"gate")

    def chunk_algebra(b, alg):
        for key in ("ar_t", "vv", "t_mt", "gam_col", "akv", "y_k"):
            alg[key] = []
        for lo in range(0, len(cgs), CHAINS_PER_STAGE):
            part = cgs[lo:lo + CHAINS_PER_STAGE]
            ar_t = [jnp.concatenate([at_ref[b, _rows_of(c), lanes_of(g)],
                                     rt_ref[b, _rows_of(c), lanes_of(g)]], axis=0) for c, g in part]
            vv = [vb_ref[b, _rows_of(c), lanes_of(g)] for c, g in part]
            gram = [_dot_nt(ar_t[i], jnp.concatenate([blk(bt_ref[b, _rows_of(c), lanes_of(g)]),
                                                      blk(kt_ref[b, _rows_of(c), lanes_of(g)])], axis=0))
                    for i, (c, g) in enumerate(part)]
            yield
            l_ab = [jnp.where(strict, m[:CHUNK, :GROUP_LANES], 0.0) for m in gram]
            l_ak = [jnp.where(strict, m[:CHUNK, GROUP_LANES:], 0.0).astype(BF16) for m in gram]
            m_rb = [jnp.where(causal, m[CHUNK:, :GROUP_LANES], 0.0).astype(BF16) for m in gram]
            m_rk = [jnp.where(causal, m[CHUNK:, GROUP_LANES:], 0.0).astype(BF16) for m in gram]

            t_inv = [eye + m for m in l_ab]
            pw = [_dot(m.astype(BF16), blk(m)) for m in l_ab]
            yield
            for _ in range(int(math.log2(CHUNK)) - 2):
                both = [_dot(jnp.concatenate([t, p], axis=0).astype(BF16), blk(p)) for t, p in zip(t_inv, pw)]
                yield
                t_inv = [t + m[:CHUNK] for t, m in zip(t_inv, both)]
                pw = [m[CHUNK:] for m in both]
            t_inv = [t + _dot(t.astype(BF16), blk(p)) for t, p in zip(t_inv, pw)]
            yield

            akv_yk = [_dot(jnp.concatenate([l_ak[i], m_rk[i]], axis=0), blk(vv[i])) for i in range(len(part))]
            yield
            t_mt = [jnp.concatenate([t_inv[i], _dot(m_rb[i], blk(t_inv[i]))], axis=0).astype(BF16)
                    for i in range(len(part))]
            yield
            gam_col = []
            for c, g in part:
                gam_row = jnp.broadcast_to(gam_ref[b, c * CHUNK:c * CHUNK + 1, lanes_of(g)],
                                           (GROUP_LANES, GROUP_LANES))
                gam_col.append(jnp.sum(jnp.where(diag, gam_row, 0.0), axis=1, keepdims=True))
            alg["ar_t"] += ar_t
            alg["vv"] += vv
            alg["t_mt"] += t_mt
            alg["gam_col"] += gam_col
            alg["akv"] += [m[:CHUNK] for m in akv_yk]
            alg["y_k"] += [m[CHUNK:] for m in akv_yk]

    def state_scan(alg):
        chains = [(b, g) for b in range(nb) for g in range(ng)]
        for c in range(nc):
            h = [h_ref[b * ng + g] for b, g in chains]
            wh = [_dot(alg[b]["ar_t"][c * ng + g], h[j].astype(BF16)) for j, (b, g) in enumerate(chains)]
            z = [wh[j][:CHUNK] + alg[b]["akv"][c * ng + g] for j, (b, g) in enumerate(chains)]
            uy = [_dot(alg[b]["t_mt"][c * ng + g], blk(z[j])) for j, (b, g) in enumerate(chains)]
            upd = [_dot_tn(jnp.concatenate([bh_ref[b, _rows_of(c), lanes_of(g)],
                                            kh_ref[b, _rows_of(c), lanes_of(g)]], axis=0),
                           jnp.concatenate([uy[j][:CHUNK].astype(BF16), alg[b]["vv"][c * ng + g]], axis=0))
                   for j, (b, g) in enumerate(chains)]
            for j, (b, g) in enumerate(chains):
                i = c * ng + g
                h_ref[b * ng + g] = h[j] * alg[b]["gam_col"][i] + jnp.where(bd_mask, upd[j], 0.0)
                y_ref[b, _rows_of(c), lanes_of(g)] = wh[j][CHUNK:] + alg[b]["y_k"][i] + uy[j][CHUNK:]

    def output(b, e):
        y = y_ref[b]
        mean = head_sum(y) * (1.0 / RWKV_HEAD)
        yield
        yc = y - mean
        var = head_sum(yc * yc) * (1.0 / RWKV_HEAD)
        yield
        y = yc * lax.rsqrt(var + GN_EPS) * lnw_ref[...] + lnb_ref[...]
        y = (y + e["bonus"]) * (e["gate"] * _sigmoid(e["gate"]))
        out = _dot(y.astype(BF16), wout_ref[...])
        yield
        o_ref[b] = x_ref[b] + _rmsnorm(out, post_g_ref[...])

    proj = [dict() for _ in range(nb)]
    elem = [dict() for _ in range(nb)]
    alg = [dict() for _ in range(nb)]
    _software_pipeline(nb, [lambda b: project(b, proj[b]),
                            lambda b: elementwise(b, proj[b], elem[b]),
                            lambda b: chunk_algebra(b, alg[b])])
    state_scan(alg)
    _interleave([output(b, elem[b]) for b in range(nb)])


def _rwkv_layer(x, pre_g, w_in, mu, w0, w_decay_up, a0, w_iclr_up, k_k, k_a, r_k, ln_w, ln_b, w_out, post_g):
    B, T, D = x.shape
    width = w_out.shape[0]
    heads = width // RWKV_HEAD
    rank2 = 2 * RWKV_RANK
    zeros = jnp.zeros((RWKV_RANK, width), F32)
    wup = jnp.concatenate([jnp.concatenate([w_decay_up, zeros], axis=1),
                           jnp.concatenate([zeros, w_iclr_up], axis=1)], axis=0).astype(BF16)
    row = lambda t: t.reshape(1, -1)
    args = (x, row(pre_g), w_in.astype(BF16), row(mu), wup, row(w0), row(a0),
            row(k_k), row(k_a), row(r_k), row(ln_w), row(ln_b), w_out.astype(BF16), row(post_g))
    tile_spec = pl.BlockSpec((B, TILE_T, D), lambda t: (0, t, 0))
    in_specs = [tile_spec] + [_const_spec(a.shape) for a in args[1:]]
    ngroups = width // GROUP_LANES
    scratch = [pltpu.VMEM((B * ngroups, GROUP_LANES, GROUP_LANES), F32),
               pltpu.VMEM((B, 1, w_in.shape[1]), F32)]
    scratch += [pltpu.VMEM((B, TILE_T, width), BF16)] * 7
    scratch += [pltpu.VMEM((B, TILE_T, width), F32)] * 2
    assert heads * RWKV_HEAD == width and rank2 == LANES_V7X
    return pl.pallas_call(
        _rwkv_kernel,
        grid=(T // TILE_T,),
        in_specs=in_specs,
        out_specs=tile_spec,
        out_shape=jax.ShapeDtypeStruct((B, T, D), x.dtype),
        scratch_shapes=scratch,
        compiler_params=pltpu.CompilerParams(
            dimension_semantics=("arbitrary",),
            vmem_limit_bytes=VMEM_LIMIT_BYTES),
        name="rwkv_layer",
    )(*args)


def kernel(x, gla_pre_norm, gla_w_in, gla_w_alpha_up, gla_b_alpha, gla_head_norm, gla_w_out, gla_post_norm,
           rwkv_pre_norm, rwkv_w_in, rwkv_mu, rwkv_w0, rwkv_w_decay_up, rwkv_a0, rwkv_w_iclr_up, rwkv_k_k,
           rwkv_k_a, rwkv_r_k, rwkv_ln_w, rwkv_ln_b, rwkv_w_out, rwkv_post_norm):
    depth = gla_pre_norm.shape[0] + rwkv_pre_norm.shape[0]
    for i in range(depth):
        j = i // 2
        if i % 2 == 0:
            x = _gla_layer(x, gla_pre_norm[j], gla_w_in[j], gla_w_alpha_up[j], gla_b_alpha[j],
                           gla_head_norm[j], gla_w_out[j], gla_post_norm[j])
        else:
            x = _rwkv_layer(x, rwkv_pre_norm[j], rwkv_w_in[j], rwkv_mu[j], rwkv_w0[j], rwkv_w_decay_up[j],
                            rwkv_a0[j], rwkv_w_iclr_up[j], rwkv_k_k[j], rwkv_k_a[j], rwkv_r_k[j].reshape(-1),
                            rwkv_ln_w[j], rwkv_ln_b[j], rwkv_w_out[j], rwkv_post_norm[j])
    return x
```
